```python
import math
import jax, jax.numpy as jnp
from jax import lax
import numpy as np

D_MODEL = 4096
BATCH = 4
SEQ = 2048
DEPTH = 4
DEC_BATCH = 8
DEC_SEQ = 4
PAST_LEN = 8192
PAGE_SIZE = 128

A_WIDTH = D_MODEL // 4
A_HEAD = 64
A_HEADS = A_WIDTH // A_HEAD
A_DECAY_R = max(32, int(round(1.8 * A_WIDTH ** 0.5 / 32)) * 32)
A_ICLR_R = max(32, int(round(1.8 * A_WIDTH ** 0.5 / 32)) * 32)
A_GATE_R = max(32, int(round(0.6 * A_WIDTH ** 0.8 / 32)) * 32)
A_COLS = 3 * A_WIDTH + A_DECAY_R + A_ICLR_R + A_GATE_R
A_SPLITS = (A_WIDTH, 2 * A_WIDTH, 3 * A_WIDTH, 3 * A_WIDTH + A_DECAY_R, 3 * A_WIDTH + A_DECAY_R + A_ICLR_R)
A_GN_EPS = 64e-5
B_WIDTH = D_MODEL // 2
B_HEADS = 16
B_VDIM = B_WIDTH // B_HEADS
B_QK = B_VDIM // 2
B_ROT = B_QK // 4
B_COLS = 3 * B_WIDTH
ROPE_THETA = 500000.0
Q_BLOCK = 128
C_WIDTH = D_MODEL - A_WIDTH - B_WIDTH
C_HEADS = 8
C_DK = 128
C_DV = C_WIDTH // C_HEADS
C_FK = C_HEADS * C_DK
C_COLS = 2 * C_FK + 2 * C_WIDTH
C_SPLITS = (C_FK, 2 * C_FK, 2 * C_FK + C_WIDTH)
C_CHUNK = 16
N_IN = A_COLS + B_COLS + C_COLS
D_FF = 11008
CONV_W = 3
EPS = 1e-5

kernel_name = 'hybrid_rwkv7_diffattn_hgrn2_convffn_step'


def rmsnorm(x, w, eps=EPS):
    xf = x.astype(jnp.float32)
    y = xf * lax.rsqrt(jnp.mean(xf * xf, axis=-1, keepdims=True) + eps)
    return (y * w.astype(jnp.float32)).astype(x.dtype)


def rope_partial(x, pos):
    half = B_ROT // 2
    inv = ROPE_THETA ** (-jnp.arange(half, dtype=jnp.float32) * 2.0 / B_ROT)
    ang = pos.astype(jnp.float32)[:, None] * inv[None, :]
    cos, sin = jnp.cos(ang)[:, None, :], jnp.sin(ang)[:, None, :]
    xr = x[..., :B_ROT].astype(jnp.float32)
    x1, x2 = xr[..., :half], xr[..., half:]
    rot = jnp.concatenate([x1 * cos - x2 * sin, x2 * cos + x1 * sin], axis=-1).astype(x.dtype)
    return jnp.concatenate([rot, x[..., B_ROT:]], axis=-1)


def rwkv7_mix(ua, prev, s0, mu, w0, w_up, a0, a_up, g_up, k_k, k_a, r_k, ln_w, ln_b):
    f32 = jnp.float32
    Bn, L = ua.shape[:2]
    u = ua.astype(f32)
    shifted = jnp.concatenate([prev.astype(f32)[:, None], u[:, :-1]], axis=1)
    xm = u + (shifted - u) * mu.astype(f32)
    r, k, v, wd, ad, gd = jnp.split(xm, A_SPLITS, axis=-1)
    w = -jax.nn.softplus(-(w0.astype(f32) + jnp.tanh(wd) @ w_up.astype(f32))) - 0.5
    decay = jnp.exp(-jnp.exp(w))
    a = jax.nn.sigmoid(a0.astype(f32) + ad @ a_up.astype(f32))
    g = jax.nn.sigmoid(gd) @ g_up.astype(f32)
    hs = lambda t: t.astype(f32).reshape(Bn, L, A_HEADS, A_HEAD)
    kk = hs(k * k_k.astype(f32))
    kk = kk / jnp.maximum(jnp.linalg.norm(kk, axis=-1, keepdims=True), 1e-12)
    k = hs(k * (1.0 + (a - 1.0) * k_a.astype(f32)))
    r, v, a, decay = hs(r), hs(v), hs(a), hs(decay)

    def step(S, inp):
        r_t, d_t, k_t, v_t, kk_t, a_t = inp
        sa = jnp.einsum('bhij,bhj->bhi', S, -kk_t)
        S = S * d_t[:, :, None, :] + sa[..., None] * (kk_t * a_t)[:, :, None, :] + v_t[..., None] * k_t[:, :, None, :]
        return S, jnp.einsum('bhij,bhj->bhi', S, r_t)

    tm = lambda t: jnp.moveaxis(t, 1, 0)
    s_fin, o = lax.scan(step, s0.astype(f32), (tm(r), tm(decay), tm(k), tm(v), tm(kk), tm(a)))
    o = jnp.moveaxis(o, 0, 1)
    mean = jnp.mean(o, axis=-1, keepdims=True)
    var = jnp.mean(jnp.square(o - mean), axis=-1, keepdims=True)
    o = ((o - mean) * lax.rsqrt(var + A_GN_EPS)).reshape(Bn, L, A_WIDTH) * ln_w.astype(f32) + ln_b.astype(f32)
    bonus = (jnp.sum(r * k * r_k.astype(f32), axis=-1, keepdims=True) * v).reshape(Bn, L, A_WIDTH)
    return (o + bonus) * g, s_fin.astype(ua.dtype), ua[:, -1]


def diff_core(q, k, v, lam, mask):
    s = jnp.einsum('bqhcd,bkhcd->bhcqk', q, k).astype(jnp.float32) * (B_QK ** -0.5)
    p = jax.nn.softmax(jnp.where(mask, s, -jnp.inf), axis=-1)
    attn = p[:, :, 0] - lam * p[:, :, 1]
    return jnp.einsum('bhqk,bkhd->bqhd', attn.astype(v.dtype), v)


def diff_attn_prompt(q, k, v, lam):
    Bn, L = q.shape[:2]
    nb = L // Q_BLOCK
    qb = q.reshape(Bn, nb, Q_BLOCK, B_HEADS, 2, B_QK).transpose(1, 0, 2, 3, 4, 5)
    kpos = jnp.arange(L)

    def block(args):
        qi, i = args
        qpos = i * Q_BLOCK + jnp.arange(Q_BLOCK)
        return diff_core(qi, k, v, lam, kpos[None, :] <= qpos[:, None])

    o = lax.map(block, (qb, jnp.arange(nb)))
    return o.transpose(1, 0, 2, 3, 4).reshape(Bn, L, B_HEADS, B_VDIM)


def hgrn2_mix(uc, s0, lb, c_norm):
    f32 = jnp.float32
    Bn, L = uc.shape[:2]
    q, f, i, g = jnp.split(uc, C_SPLITS, axis=-1)
    q = jax.nn.silu(q.astype(f32))
    fs = lb + (1.0 - lb) * jax.nn.sigmoid(f.astype(f32))
    kin = 1.0 - fs
    gl = jnp.log(fs)
    n_chunks = -(-L // C_CHUNK)
    pad = n_chunks * C_CHUNK - L

    def chunks(t, d):
        t = jnp.pad(t.astype(f32), ((0, 0), (0, pad), (0, 0)))
        return t.reshape(Bn, n_chunks, C_CHUNK, C_HEADS, d).transpose(1, 0, 3, 2, 4)

    tri = jnp.tril(jnp.ones((C_CHUNK, C_CHUNK), dtype=bool))[..., None]

    def step(S, inp):
        qc, kc, gc, ic = inp
        b = jnp.cumsum(gc, axis=2)
        o = jnp.einsum('bhtk,bhkv->bhtv', qc * jnp.exp(b), S)
        dec = jnp.exp(jnp.where(tri, b[:, :, :, None] - b[:, :, None], -jnp.inf))
        att = jnp.einsum('bhtk,bhsk,bhtsk->bhts', qc, kc, dec)
        o = o + jnp.einsum('bhts,bhsv->bhtv', att, ic)
        bl = b[:, :, -1:]
        S = jnp.exp(bl[:, :, 0])[..., None] * S + jnp.einsum('bhsk,bhsv->bhkv', kc * jnp.exp(bl - b), ic)
        return S, o

    s_fin, o = lax.scan(step, s0.astype(f32), (chunks(q, C_DK), chunks(kin, C_DK), chunks(gl, C_DK), chunks(i, C_DV)))
    o = o.transpose(1, 0, 3, 2, 4).reshape(Bn, n_chunks * C_CHUNK, C_HEADS, C_DV)[:, :L]
    o = rmsnorm(o, c_norm).reshape(Bn, L, C_WIDTH) * jax.nn.silu(g.astype(f32))
    return o, s_fin.astype(uc.dtype)


def conv_ffn(hn, buf, w_gate, w_up, w_conv, b_conv, w_down):
    L = hn.shape[1]
    gate = hn @ w_gate
    ext = jnp.concatenate([buf.astype(gate.dtype), gate], axis=1)
    conv = b_conv + ext[:, 0:L] * w_conv[0]
    for j in range(1, CONV_W):
        conv = conv + ext[:, j:j + L] * w_conv[j]
    y = (jax.nn.silu(conv) * (hn @ w_up)) @ w_down
    return y, ext[:, L:]


def run_trunk(x, pos, cache_k, cache_v, page_table, wkv0, shift0, hgrn0, conv0, lower_bounds, P):
    Bn, L = x.shape[:2]
    h = x
    ks, vs, wkvs, shifts, hgrns, convs = [], [], [], [], [], []
    for l in range(DEPTH):
        hn = rmsnorm(h, P['norm_mix'][l])
        u = hn @ P['w_in'][l]
        ua, ub, uc = u[..., :A_COLS], u[..., A_COLS:A_COLS + B_COLS], u[..., A_COLS + B_COLS:]
        oa, s_a, last_a = rwkv7_mix(ua, shift0[l], wkv0[l], P['a_mu'][l], P['a_w0'][l], P['a_w_up'][l],
                                    P['a_a0'][l], P['a_a_up'][l], P['a_g_up'][l], P['a_k_k'][l], P['a_k_a'][l],
                                    P['a_r_k'][l], P['a_ln_w'][l], P['a_ln_b'][l])
        qb, kb, vb = ub[..., :B_WIDTH], ub[..., B_WIDTH:2 * B_WIDTH], ub[..., 2 * B_WIDTH:]
        qb = rope_partial(qb.reshape(Bn, L, 2 * B_HEADS, B_QK), pos).reshape(Bn, L, B_HEADS, 2, B_QK)
        kb = rope_partial(kb.reshape(Bn, L, 2 * B_HEADS, B_QK), pos).reshape(Bn, L, B_HEADS, 2, B_QK)
        vb = vb.reshape(Bn, L, B_HEADS, B_VDIM)
        lam_init = 0.8 - 0.6 * math.exp(-0.3 * l)
        lam = (jnp.exp(jnp.sum(P['b_lq1'][l].astype(jnp.float32) * P['b_lk1'][l].astype(jnp.float32)))
               - jnp.exp(jnp.sum(P['b_lq2'][l].astype(jnp.float32) * P['b_lk2'][l].astype(jnp.float32))) + lam_init)
        if cache_k is None:
            ob = diff_attn_prompt(qb, kb, vb, lam)
        else:
            pk = cache_k[l][page_table].reshape(Bn, -1, B_HEADS, 2, B_QK).astype(kb.dtype)
            pv = cache_v[l][page_table].reshape(Bn, -1, B_HEADS, B_VDIM).astype(vb.dtype)
            past = pk.shape[1]
            mask = jnp.arange(past + L)[None, :] <= (past + jnp.arange(L))[:, None]
            ob = diff_core(qb, jnp.concatenate([pk, kb], axis=1), jnp.concatenate([pv, vb], axis=1), lam, mask)
        ob = (rmsnorm(ob, P['b_subln'][l]) * (1.0 - lam_init)).reshape(Bn, L, B_WIDTH)
        oc, s_c = hgrn2_mix(uc, hgrn0[l], lower_bounds[l], P['c_norm'][l])
        mix = jnp.concatenate([oa.astype(h.dtype), ob.astype(h.dtype), oc.astype(h.dtype)], axis=-1)
        h = h + mix @ P['w_out'][l]
        f_out, buf = conv_ffn(rmsnorm(h, P['norm_ffn'][l]), conv0[l], P['f_gate'][l], P['f_up'][l],
                              P['f_conv'][l], P['f_conv_b'][l], P['f_down'][l])
        h = h + f_out.astype(h.dtype)
        ks.append(kb.reshape(Bn, L, B_HEADS, 2 * B_QK))
        vs.append(vb)
        wkvs.append(s_a)
        shifts.append(last_a)
        hgrns.append(s_c)
        convs.append(buf)
    y = rmsnorm(h, P['norm_final'])
    return y, jnp.stack(ks), jnp.stack(vs), jnp.stack(wkvs), jnp.stack(shifts), jnp.stack(hgrns), jnp.stack(convs)


def setup_inputs(seed: int = 0) -> dict:
    key = jax.random.key(seed)
    ks = iter(jax.random.split(key, 48))
    f32 = jnp.float32
    nrm = lambda shape, s=1.0: jax.random.normal(next(ks), shape, f32) * s
    n_pages = PAST_LEN // PAGE_SIZE
    n_pool = (DEC_BATCH * n_pages * 5) // 4
    out = {}
    out['x_prompt'] = nrm((BATCH, SEQ, D_MODEL))
    out['x_sample'] = nrm((DEC_BATCH, DEC_SEQ, D_MODEL))
    out['cache_k'] = nrm((DEPTH, n_pool, PAGE_SIZE, B_HEADS, 2 * B_QK))
    out['cache_v'] = nrm((DEPTH, n_pool, PAGE_SIZE, B_HEADS, B_VDIM))
    out['page_table'] = jax.random.permutation(next(ks), n_pool)[:DEC_BATCH * n_pages].reshape(DEC_BATCH, n_pages).astype(jnp.int32)
    out['state_wkv'] = nrm((DEPTH, DEC_BATCH, A_HEADS, A_HEAD, A_HEAD), 0.5)
    out['state_shift'] = nrm((DEPTH, DEC_BATCH, A_COLS))
    out['state_hgrn'] = nrm((DEPTH, DEC_BATCH, C_HEADS, C_DK, C_DV), 0.5)
    out['state_conv'] = nrm((DEPTH, DEC_BATCH, CONV_W - 1, D_FF))
    out['norm_mix'] = 1.0 + nrm((DEPTH, D_MODEL), 0.05)
    out['w_in'] = nrm((DEPTH, D_MODEL, N_IN), D_MODEL ** -0.5)
    out['a_mu'] = jax.random.uniform(next(ks), (DEPTH, A_COLS), f32)
    out['a_w0'] = jax.random.uniform(next(ks), (DEPTH, A_WIDTH), f32, minval=-6.0, maxval=1.0)
    out['a_w_up'] = nrm((DEPTH, A_DECAY_R, A_WIDTH), A_DECAY_R ** -0.5)
    out['a_a0'] = nrm((DEPTH, A_WIDTH), 0.5)
    out['a_a_up'] = nrm((DEPTH, A_ICLR_R, A_WIDTH), A_ICLR_R ** -0.5)
    out['a_g_up'] = nrm((DEPTH, A_GATE_R, A_WIDTH), A_GATE_R ** -0.5)
    out['a_k_k'] = 0.85 + nrm((DEPTH, A_WIDTH), 0.05)
    out['a_k_a'] = 1.0 + nrm((DEPTH, A_WIDTH), 0.05)
    out['a_r_k'] = nrm((DEPTH, A_HEADS, A_HEAD), 0.3)
    out['a_ln_w'] = 1.0 + nrm((DEPTH, A_WIDTH), 0.05)
    out['a_ln_b'] = nrm((DEPTH, A_WIDTH), 0.02)
    out['b_lq1'] = nrm((DEPTH, B_QK), 0.1)
    out['b_lk1'] = nrm((DEPTH, B_QK), 0.1)
    out['b_lq2'] = nrm((DEPTH, B_QK), 0.1)
    out['b_lk2'] = nrm((DEPTH, B_QK), 0.1)
    out['b_subln'] = 1.0 + nrm((DEPTH, B_VDIM), 0.05)
    out['c_lb_logits'] = 1.0 + nrm((DEPTH, C_FK), 0.1)
    out['c_norm'] = 1.0 + nrm((DEPTH, C_DV), 0.05)
    out['w_out'] = nrm((DEPTH, D_MODEL, D_MODEL), D_MODEL ** -0.5)
    out['norm_ffn'] = 1.0 + nrm((DEPTH, D_MODEL), 0.05)
    out['f_gate'] = nrm((DEPTH, D_MODEL, D_FF), D_MODEL ** -0.5)
    out['f_up'] = nrm((DEPTH, D_MODEL, D_FF), D_MODEL ** -0.5)
    out['f_conv'] = nrm((DEPTH, CONV_W, D_FF), CONV_W ** -0.5)
    out['f_conv_b'] = nrm((DEPTH, D_FF), 0.02)
    out['f_down'] = nrm((DEPTH, D_FF, D_MODEL), D_FF ** -0.5)
    out['norm_final'] = 1.0 + nrm((D_MODEL,), 0.05)
    return out


def reference(x_prompt, x_sample, cache_k, cache_v, page_table, state_wkv, state_shift, state_hgrn, state_conv,
              norm_mix, w_in, a_mu, a_w0, a_w_up, a_a0, a_a_up, a_g_up, a_k_k, a_k_a, a_r_k, a_ln_w, a_ln_b,
              b_lq1, b_lk1, b_lq2, b_lk2, b_subln, c_lb_logits, c_norm, w_out, norm_ffn, f_gate, f_up,
              f_conv, f_conv_b, f_down, norm_final):
    P = dict(norm_mix=norm_mix, w_in=w_in, a_mu=a_mu, a_w0=a_w0, a_w_up=a_w_up, a_a0=a_a0, a_a_up=a_a_up,
             a_g_up=a_g_up, a_k_k=a_k_k, a_k_a=a_k_a, a_r_k=a_r_k, a_ln_w=a_ln_w, a_ln_b=a_ln_b,
             b_lq1=b_lq1, b_lk1=b_lk1, b_lq2=b_lq2, b_lk2=b_lk2, b_subln=b_subln, c_norm=c_norm,
             w_out=w_out, norm_ffn=norm_ffn, f_gate=f_gate, f_up=f_up, f_conv=f_conv, f_conv_b=f_conv_b,
             f_down=f_down, norm_final=norm_final)
    lb_soft = jax.nn.softmax(c_lb_logits.astype(jnp.float32), axis=0)
    lower_bounds = jnp.cumsum(lb_soft, axis=0) - lb_soft[0]
    dt = x_prompt.dtype
    bp, lp = x_prompt.shape[:2]
    pos_p = jnp.arange(lp)
    past_len = page_table.shape[1] * cache_k.shape[2]
    pos_s = past_len + jnp.arange(x_sample.shape[1])
    y_p, k_p, v_p, wkv_p, sh_p, hg_p, cv_p = run_trunk(
        x_prompt, pos_p, None, None, None,
        jnp.zeros((DEPTH, bp, A_HEADS, A_HEAD, A_HEAD), dt), jnp.zeros((DEPTH, bp, A_COLS), dt),
        jnp.zeros((DEPTH, bp, C_HEADS, C_DK, C_DV), dt), jnp.zeros((DEPTH, bp, CONV_W - 1, D_FF), dt),
        lower_bounds, P)
    y_s, k_s, v_s, wkv_s, sh_s, hg_s, cv_s = run_trunk(
        x_sample, pos_s, cache_k, cache_v, page_table, state_wkv, state_shift, state_hgrn, state_conv,
        lower_bounds, P)
    return (y_p, y_s, k_p, v_p, k_s, v_s, wkv_p, wkv_s, sh_p, sh_s, hg_p, hg_s, cv_p, cv_s)
```

```python
import functools
import math

import jax
import jax.numpy as jnp
from jax import lax
from jax.experimental import pallas as pl
from jax.experimental.pallas import tpu as pltpu

F32 = jnp.float32
BF16 = jnp.bfloat16

V7X_VMEM_BYTES = 64 * 1024 * 1024
VMEM_LIMIT = V7X_VMEM_BYTES - 8 * 1024 * 1024
LANES = 128
SUBLANES = 8

D_MODEL = 4096
DEPTH = 4
A_WIDTH = D_MODEL // 4
A_HEAD = 64
A_HEADS = A_WIDTH // A_HEAD
A_PAIRS = A_HEADS // 2
A_DECAY_R = 64
A_ICLR_R = 64
A_GATE_R = 160
A_COLS = 3 * A_WIDTH + A_DECAY_R + A_ICLR_R + A_GATE_R
A_COLS_PAD = 3 * A_WIDTH + 3 * LANES
A_GN_EPS = 64e-5
B_WIDTH = D_MODEL // 2
B_HEADS = 16
B_VDIM = B_WIDTH // B_HEADS
B_QK = B_VDIM // 2
B_ROT = B_QK // 4
ROPE_THETA = 500000.0
C_WIDTH = D_MODEL - A_WIDTH - B_WIDTH
C_HEADS = 8
C_DK = 128
C_DV = C_WIDTH // C_HEADS
C_FK = C_HEADS * C_DK
C_CHUNK = 16
D_FF = 11008
CONV_W = 3
EPS = 1e-5
NEG_INF = float("-inf")


def _cparams(sem):
    return pltpu.CompilerParams(dimension_semantics=sem, vmem_limit_bytes=VMEM_LIMIT)


def _sigmoid(x):
    return 1.0 / (1.0 + jnp.exp(-x))


def _silu(x):
    return x * _sigmoid(x)


def _block_ones(dtype):
    r = lax.broadcasted_iota(jnp.int32, (LANES, LANES), 0) // A_HEAD
    c = lax.broadcasted_iota(jnp.int32, (LANES, LANES), 1) // A_HEAD
    return (r == c).astype(dtype)


def _rmsnorm_body(x_ref, w_ref, o_ref):
    x = x_ref[...]
    y = x * lax.rsqrt(jnp.mean(x * x, axis=-1, keepdims=True) + EPS)
    o_ref[...] = (y * w_ref[...]).astype(o_ref.dtype)


def rmsnorm_rows(x, w, out_dtype, tr=256):
    t, d = x.shape
    tr = min(tr, t)
    return pl.pallas_call(
        _rmsnorm_body,
        grid=(pl.cdiv(t, tr),),
        in_specs=[pl.BlockSpec((tr, d), lambda i: (i, 0)), pl.BlockSpec((1, d), lambda i: (0, 0))],
        out_specs=pl.BlockSpec((tr, d), lambda i: (i, 0)),
        out_shape=jax.ShapeDtypeStruct((t, d), out_dtype),
        compiler_params=_cparams(("parallel",)),
        name="rmsnorm_rows",
    )(x, w.reshape(1, d))


def _mm_body(*refs, nk, tk, k_total, has_res):
    if has_res:
        a_ref, w_ref, r_ref, o_ref = refs
    else:
        a_ref, w_ref, o_ref = refs
        r_ref = None
    a = a_ref[...]
    w = w_ref[...].astype(BF16)
    if nk == 1:
        acc = jnp.dot(a, w, preferred_element_type=F32)
        if has_res:
            acc = acc + r_ref[...]
        o_ref[...] = acc.astype(o_ref.dtype)
        return
    k = pl.program_id(2)
    if k_total % tk:
        valid = k_total - k * tk
        a = jnp.where(lax.broadcasted_iota(jnp.int32, a.shape, 1) < valid, a, jnp.zeros_like(a))
        w = jnp.where(lax.broadcasted_iota(jnp.int32, w.shape, 0) < valid, w, jnp.zeros_like(w))
    part = jnp.dot(a, w, preferred_element_type=F32)

    @pl.when(k == 0)
    def _():
        o_ref[...] = (part + r_ref[...]) if has_res else part

    @pl.when(k > 0)
    def _():
        o_ref[...] += part


def matmul(a, w, res=None, out_dtype=F32, tm=2048, tn=256, tk=None):
    m, kdim = a.shape
    n = w.shape[1]
    tm = min(tm, m)
    tk = kdim if tk is None else tk
    nk = pl.cdiv(kdim, tk)
    if nk > 1:
        assert out_dtype == F32
    grid = (pl.cdiv(m, tm), pl.cdiv(n, tn), nk)
    a_kw = {}
    if nk == 1 and grid[1] > 1:
        a_kw = dict(pipeline_mode=pl.Buffered(1))
    in_specs = [pl.BlockSpec((tm, tk), lambda i, j, k: (i, k), **a_kw),
                pl.BlockSpec((tk, tn), lambda i, j, k: (k, j))]
    args = [a, w]
    if res is not None:
        in_specs.append(pl.BlockSpec((tm, tn), lambda i, j, k: (i, j)))
        args.append(res)
    return pl.pallas_call(
        functools.partial(_mm_body, nk=nk, tk=tk, k_total=kdim, has_res=res is not None),
        grid=grid,
        in_specs=in_specs,
        out_specs=pl.BlockSpec((tm, tn), lambda i, j, k: (i, j)),
        out_shape=jax.ShapeDtypeStruct((m, n), out_dtype),
        compiler_params=_cparams(("parallel", "parallel", "arbitrary")),
        name="matmul",
    )(*args)


def _ffn_up_body(a_ref, wg_ref, wu_ref, cw_ref, cb_ref, buf_ref, o_ref, nbuf_ref, ext_ref, *, nb, seq_len):
    tm = nb * seq_len
    a = a_ref[...]
    gate = jnp.dot(a, wg_ref[...].astype(BF16), preferred_element_type=F32)
    up = jnp.dot(a, wu_ref[...].astype(BF16), preferred_element_type=F32)
    ext_ref[0:SUBLANES, :] = jnp.zeros((SUBLANES, gate.shape[1]), F32)
    ext_ref[SUBLANES:SUBLANES + tm, :] = gate
    g1 = ext_ref[SUBLANES - 1:SUBLANES - 1 + tm, :]
    g2 = ext_ref[SUBLANES - 2:SUBLANES - 2 + tm, :]
    row = lax.broadcasted_iota(jnp.int32, gate.shape, 0)
    for s in range(nb):
        b0 = buf_ref[s, 0:1, :]
        b1 = buf_ref[s, 1:2, :]
        g1 = jnp.where(row == s * seq_len, b1, g1)
        g2 = jnp.where(row == s * seq_len, b0, g2)
        g2 = jnp.where(row == s * seq_len + 1, b1, g2)
        last = SUBLANES + (s + 1) * seq_len
        nbuf_ref[s, :, :] = ext_ref[last - 2:last, :]
    conv = cb_ref[...] + g2 * cw_ref[0:1, :] + g1 * cw_ref[1:2, :] + gate * cw_ref[2:3, :]
    o_ref[...] = (_silu(conv) * up).astype(o_ref.dtype)


def ffn_up(hn, w_gate, w_up, w_conv, b_conv, buf, seq_len, tn=256):
    t, d = hn.shape
    n = w_gate.shape[1]
    nseq = buf.shape[0]
    nb = 1 if seq_len >= 256 else nseq
    tm = nb * seq_len
    assert seq_len >= 2 and t == nseq * seq_len and n % tn == 0
    grid = (t // tm, n // tn)
    return pl.pallas_call(
        functools.partial(_ffn_up_body, nb=nb, seq_len=seq_len),
        grid=grid,
        in_specs=[
            pl.BlockSpec((tm, d), lambda i, j: (i, 0), pipeline_mode=pl.Buffered(1)),
            pl.BlockSpec((d, tn), lambda i, j: (0, j)),
            pl.BlockSpec((d, tn), lambda i, j: (0, j)),
            pl.BlockSpec((CONV_W, tn), lambda i, j: (0, j)),
            pl.BlockSpec((1, tn), lambda i, j: (0, j)),
            pl.BlockSpec((nb, CONV_W - 1, tn), lambda i, j: (i, 0, j)),
        ],
        out_specs=[
            pl.BlockSpec((tm, tn), lambda i, j: (i, j)),
            pl.BlockSpec((nb, CONV_W - 1, tn), lambda i, j: (i, 0, j)),
        ],
        out_shape=[
            jax.ShapeDtypeStruct((t, n), BF16),
            jax.ShapeDtypeStruct((nseq, CONV_W - 1, n), F32),
        ],
        scratch_shapes=[pltpu.VMEM((SUBLANES + tm, tn), F32)],
        compiler_params=_cparams(("parallel", "parallel")),
        name="ffn_up",
    )(hn, w_gate, w_up, w_conv, b_conv.reshape(1, n), buf)


def _lower_bounds_body(x_ref, o_ref):
    x = x_ref[...]
    e = jnp.exp(x - jnp.max(x, axis=0, keepdims=True))
    p = e / jnp.sum(e, axis=0, keepdims=True)
    run = jnp.zeros_like(p[0:1, :])
    for l in range(x.shape[0]):
        run = run + p[l:l + 1, :]
        o_ref[l:l + 1, :] = run - p[0:1, :]


def lower_bounds(logits):
    return pl.pallas_call(
        _lower_bounds_body,
        out_shape=jax.ShapeDtypeStruct(logits.shape, F32),
        name="lower_bounds",
    )(logits)


def _seg_sum(x, ones):
    parts = []
    for s in range(x.shape[1] // LANES):
        parts.append(jnp.dot(x[:, s * LANES:(s + 1) * LANES], ones,
                             preferred_element_type=F32, precision=lax.Precision.HIGHEST))
    return jnp.concatenate(parts, axis=1)


def _rwkv_pre_body(u_ref, prev_ref, mu_ref, w2_ref, gup_ref, w0_ref, a0_ref, kk_ref, ka_ref, rk_ref,
                   r_o, d_o, k_o, v_o, kk_o, b_o, g_o, bonus_o, ext_ref, *, tm):
    j = pl.program_id(1)
    u = u_ref[0]

    @pl.when(j == 0)
    def _():
        ext_ref[SUBLANES - 1:SUBLANES, :] = prev_ref[0]

    ext_ref[SUBLANES:SUBLANES + tm, :] = u
    shifted = ext_ref[SUBLANES - 1:SUBLANES - 1 + tm, :]
    ext_ref[SUBLANES - 1:SUBLANES, :] = u[tm - 1:tm, :]
    xm = u + (shifted - u) * mu_ref[...]
    w = A_WIDTH
    r = xm[:, 0:w]
    k = xm[:, w:2 * w]
    v = xm[:, 2 * w:3 * w]
    lr = xm[:, 3 * w:3 * w + LANES]
    gd = xm[:, 3 * w + LANES:3 * w + 3 * LANES]
    lane = lax.broadcasted_iota(jnp.int32, lr.shape, 1)
    lr = jnp.where(lane < A_DECAY_R, jnp.tanh(lr), lr)
    wa = jnp.dot(lr.astype(BF16), w2_ref[...].astype(BF16), preferred_element_type=F32)
    wraw = -(w0_ref[...] + wa[:, 0:w])
    softplus = jnp.maximum(wraw, 0.0) + jnp.log(1.0 + jnp.exp(-jnp.abs(wraw)))
    decay = jnp.exp(-jnp.exp(-softplus - 0.5))
    a = _sigmoid(a0_ref[...] + wa[:, w:2 * w])
    g = jnp.dot(_sigmoid(gd).astype(BF16), gup_ref[...].astype(BF16), preferred_element_type=F32)
    ones = _block_ones(F32)
    kk = k * kk_ref[...]
    norm = jnp.sqrt(_seg_sum(kk * kk, ones))
    kk = kk / jnp.maximum(norm, 1e-12)
    kp = k * (1.0 + (a - 1.0) * ka_ref[...])
    bonus = _seg_sum(r * kp * rk_ref[...], ones) * v
    r_o[0] = r
    d_o[0] = decay
    k_o[0] = kp
    v_o[0] = v
    kk_o[0] = kk
    b_o[0] = kk * a
    g_o[0] = g
    bonus_o[0] = bonus


def rwkv_pre(ua, prev, mu, w2, gup, w0, a0, k_k, k_a, r_k, tm=256):
    bsz, seq_len, cols = ua.shape
    tm = min(tm, seq_len)
    assert seq_len % tm == 0 and cols == A_COLS_PAD
    w = A_WIDTH
    row = lambda x: x.reshape(1, -1)
    vec = lambda n: pl.BlockSpec((1, n), lambda b, j: (0, 0))
    out_spec = pl.BlockSpec((1, tm, w), lambda b, j: (b, j, 0))
    out_sds = jax.ShapeDtypeStruct((bsz, seq_len, w), F32)
    return pl.pallas_call(
        functools.partial(_rwkv_pre_body, tm=tm),
        grid=(bsz, seq_len // tm),
        in_specs=[
            pl.BlockSpec((1, tm, cols), lambda b, j: (b, j, 0)),
            pl.BlockSpec((1, 1, cols), lambda b, j: (b, 0, 0)),
            vec(cols),
            pl.BlockSpec((LANES, 2 * w), lambda b, j: (0, 0)),
            pl.BlockSpec((2 * LANES, w), lambda b, j: (0, 0)),
            vec(w), vec(w), vec(w), vec(w), vec(w),
        ],
        out_specs=[out_spec] * 8,
        out_shape=[out_sds] * 8,
        scratch_shapes=[pltpu.VMEM((SUBLANES + tm, cols), F32)],
        compiler_params=_cparams(("parallel", "arbitrary")),
        name="rwkv_pre",
    )(ua, prev.reshape(bsz, 1, cols), row(mu), w2, gup, row(w0), row(a0), row(k_k), row(k_a), row(r_k))


def _rwkv_scan_body(r_ref, d_ref, k_ref, v_ref, kk_ref, b_ref, s0_ref, o_ref, sfin_ref, s_ref, *, nbt, tc):
    c = pl.program_id(1)

    @pl.when(c == 0)
    def _():
        s_ref[...] = s0_ref[...]

    ones = _block_ones(BF16)
    ones2 = jnp.concatenate([ones, ones], axis=0)
    eye2 = (lax.broadcasted_iota(jnp.int32, (A_HEAD, LANES), 0)
            == lax.broadcasted_iota(jnp.int32, (A_HEAD, LANES), 1) % A_HEAD)
    eye2f = eye2.astype(F32)

    sub = min(SUBLANES, tc)
    rowi = lax.broadcasted_iota(jnp.int32, (sub, LANES), 0)
    bc = lambda x: jnp.broadcast_to(x, (A_HEAD, LANES))

    def group(gi, carry):
        rows = pl.ds(pl.multiple_of(gi * sub, sub), sub)
        for nb in range(nbt):
            for p in range(A_PAIRS):
                sl = pl.ds(p * LANES, LANES)
                r8, d8, k8, v8, kk8, b8 = (ref[nb, rows, sl] for ref in (r_ref, d_ref, k_ref, v_ref, kk_ref, b_ref))
                v_hi8 = v8.astype(BF16).astype(F32)
                v_lo8 = v8 - v_hi8
                s = s_ref[nb, p]
                otile = jnp.zeros((sub, LANES), F32)
                for j in range(sub):
                    row = lambda x8: x8[j:j + 1, :]
                    p1 = (s * bc(row(kk8))).astype(BF16)
                    p3 = jnp.concatenate([(eye2f * bc(row(v_hi8))).astype(BF16),
                                          (eye2f * bc(row(v_lo8))).astype(BF16)], axis=1)
                    sa = jnp.dot(p1, ones, preferred_element_type=F32)
                    vcol = jnp.dot(p3, ones2, preferred_element_type=F32)
                    s = s * bc(row(d8)) - sa * bc(row(b8)) + vcol * bc(row(k8))
                    p2 = (s * bc(row(r8))).astype(BF16)
                    ob = jnp.dot(p2, ones, preferred_element_type=F32)
                    otile = jnp.where(rowi == j, jnp.sum(ob * eye2f, axis=0, keepdims=True), otile)
                s_ref[nb, p] = s
                o_ref[nb, rows, sl] = otile
        return carry

    lax.fori_loop(0, tc // sub, group, 0)

    @pl.when(c == pl.num_programs(1) - 1)
    def _():
        sfin_ref[...] = s_ref[...]


def rwkv_scan(r, d, k, v, kk, b, s0, tc=64, nbt=2):
    bsz, seq_len, w = r.shape
    tc = min(tc, seq_len)
    nbt = min(nbt, bsz)
    assert seq_len % tc == 0 and bsz % nbt == 0 and w == A_WIDTH
    tok = pl.BlockSpec((nbt, tc, w), lambda i, c: (i, c, 0))
    st = pl.BlockSpec((nbt, A_PAIRS, A_HEAD, LANES), lambda i, c: (i, 0, 0, 0))
    return pl.pallas_call(
        functools.partial(_rwkv_scan_body, nbt=nbt, tc=tc),
        grid=(bsz // nbt, seq_len // tc),
        in_specs=[tok] * 6 + [st],
        out_specs=[tok, st],
        out_shape=[jax.ShapeDtypeStruct((bsz, seq_len, w), F32),
                   jax.ShapeDtypeStruct((bsz, A_PAIRS, A_HEAD, LANES), F32)],
        scratch_shapes=[pltpu.VMEM((nbt, A_PAIRS, A_HEAD, LANES), F32)],
        compiler_params=_cparams(("parallel", "arbitrary")),
        name="rwkv_scan",
    )(r, d, k, v, kk, b, s0)


def _rwkv_post_body(o_ref, bonus_ref, g_ref, lnw_ref, lnb_ref, out_ref):
    o = o_ref[...]
    ones = _block_ones(F32)
    inv_n = 1.0 / A_HEAD
    mean = _seg_sum(o, ones) * inv_n
    cen = o - mean
    var = _seg_sum(cen * cen, ones) * inv_n
    y = cen * lax.rsqrt(var + A_GN_EPS) * lnw_ref[...] + lnb_ref[...]
    out_ref[...] = ((y + bonus_ref[...]) * g_ref[...]).astype(out_ref.dtype)


def rwkv_post(o, bonus, g, ln_w, ln_b, tr=256):
    t, w = o.shape
    tr = min(tr, t)
    tok = pl.BlockSpec((tr, w), lambda i: (i, 0))
    vec = pl.BlockSpec((1, w), lambda i: (0, 0))
    return pl.pallas_call(
        _rwkv_post_body,
        grid=(pl.cdiv(t, tr),),
        in_specs=[tok, tok, tok, vec, vec],
        out_specs=tok,
        out_shape=jax.ShapeDtypeStruct((t, w), BF16),
        compiler_params=_cparams(("parallel",)),
        name="rwkv_post",
    )(o, bonus, g, ln_w.reshape(1, w), ln_b.reshape(1, w))


def _rope_body(x_ref, cos_ref, sa_ref, sb_ref, o_ref, *, nslab):
    cos = cos_ref[...]
    sin_a = sa_ref[...]
    sin_b = sb_ref[...]
    half = B_ROT // 2
    for s in range(nslab):
        x = x_ref[0, :, s * LANES:(s + 1) * LANES]
        y = x * cos + pltpu.roll(x, half, 1) * sin_a + pltpu.roll(x, LANES - half, 1) * sin_b
        o_ref[0, :, s * LANES:(s + 1) * LANES] = y


def rope_tables(pos):
    half = B_ROT // 2
    inv = ROPE_THETA ** (-jnp.arange(half, dtype=F32) * 2.0 / B_ROT)
    ang = pos.astype(F32)[:, None] * inv[None, :]
    cos, sin = jnp.cos(ang), jnp.sin(ang)
    n = pos.shape[0]
    one = jnp.ones((n, B_QK - B_ROT), F32)
    zero = jnp.zeros((n, B_QK - B_ROT), F32)
    zh = jnp.zeros((n, half), F32)
    comp = lambda a, b_, rest: jnp.concatenate([a, b_, rest], axis=1)
    cos_p = comp(cos, cos, one)
    sin_a = comp(zh, sin, zero)
    sin_b = comp(-sin, zh, zero)
    dup = lambda x: jnp.concatenate([x, x], axis=1)
    return dup(cos_p), dup(sin_a), dup(sin_b)


def rope_qk(ub, tables, tm=256):
    bsz, seq_len, _ = ub.shape
    tm = min(tm, seq_len)
    nslab = 2 * B_WIDTH // LANES
    tab = pl.BlockSpec((tm, LANES), lambda b, j: (j, 0))
    return pl.pallas_call(
        functools.partial(_rope_body, nslab=nslab),
        grid=(bsz, seq_len // tm),
        in_specs=[pl.BlockSpec((1, tm, 2 * B_WIDTH), lambda b, j: (b, j, 0)), tab, tab, tab],
        out_specs=pl.BlockSpec((1, tm, 2 * B_WIDTH), lambda b, j: (b, j, 0)),
        out_shape=jax.ShapeDtypeStruct((bsz, seq_len, 2 * B_WIDTH), F32),
        compiler_params=_cparams(("parallel", "parallel")),
        name="rope_qk",
    )(ub, *tables)


def _lam(lqk_ref, lam_init):
    d1 = jnp.sum(lqk_ref[0:1, :] * lqk_ref[1:2, :], axis=-1, keepdims=True)
    d2 = jnp.sum(lqk_ref[2:3, :] * lqk_ref[3:4, :], axis=-1, keepdims=True)
    return jnp.exp(d1) - jnp.exp(d2) + lam_init


def _online_update(s, v, m_ref, l_ref, acc_ref):
    m_prev = m_ref[...]
    m_new = jnp.maximum(m_prev, jnp.max(s, axis=-1, keepdims=True))
    alpha = jnp.exp(m_prev - m_new)
    p = jnp.exp(s - m_new)
    l_ref[...] = alpha * l_ref[...] + jnp.sum(p, axis=-1, keepdims=True)
    acc_ref[...] = alpha * acc_ref[...] + jnp.dot(p.astype(BF16), v, preferred_element_type=F32)
    m_ref[...] = m_new


def _diff_finish(acc1, l1, acc2, l2, lam, subln, lam_init):
    a = acc1 / l1 - lam * (acc2 / l2)
    y = a * lax.rsqrt(jnp.mean(a * a, axis=-1, keepdims=True) + EPS) * subln
    return y * (1.0 - lam_init)


_NT = (((1,), (1,)), ((), ()))


def _attn_prompt_body(q_ref, k_ref, v_ref, lqk_ref, sub_ref, o_ref,
                      qlo_ref, qhi_ref, m1_ref, l1_ref, a1_ref, m2_ref, l2_ref, a2_ref, *, tq, tk, lam_init):
    qi = pl.program_id(2)
    ki = pl.program_id(3)

    @pl.when(ki == 0)
    def _():
        q = q_ref[0] * (B_QK ** -0.5)
        lane = lax.broadcasted_iota(jnp.int32, q.shape, 1)
        qlo_ref[...] = jnp.where(lane < B_QK, q, 0.0).astype(BF16)
        qhi_ref[...] = jnp.where(lane >= B_QK, q, 0.0).astype(BF16)
        for m_ref, l_ref, a_ref in ((m1_ref, l1_ref, a1_ref), (m2_ref, l2_ref, a2_ref)):
            m_ref[...] = jnp.full(m_ref.shape, NEG_INF, F32)
            l_ref[...] = jnp.zeros(l_ref.shape, F32)
            a_ref[...] = jnp.zeros(a_ref.shape, F32)

    @pl.when(ki * tk < (qi + 1) * tq)
    def _():
        k = k_ref[0].astype(BF16)
        v = v_ref[0].astype(BF16)
        row = qi * tq + lax.broadcasted_iota(jnp.int32, (tq, tk), 0)
        col = ki * tk + lax.broadcasted_iota(jnp.int32, (tq, tk), 1)
        keep = col <= row
        for q_sc, m_ref, l_ref, a_ref in ((qlo_ref, m1_ref, l1_ref, a1_ref), (qhi_ref, m2_ref, l2_ref, a2_ref)):
            s = lax.dot_general(q_sc[...], k, _NT, preferred_element_type=F32)
            _online_update(jnp.where(keep, s, NEG_INF), v, m_ref, l_ref, a_ref)

    @pl.when(ki == pl.num_programs(3) - 1)
    def _():
        y = _diff_finish(a1_ref[...], l1_ref[...], a2_ref[...], l2_ref[...], _lam(lqk_ref, lam_init),
                         sub_ref[...], lam_init)
        o_ref[0] = y.astype(o_ref.dtype)


def attn_prompt(qk, ub, lqk, subln, lam_init, tq=512, tk=512):
    bsz, seq_len, _ = qk.shape
    tq = min(tq, seq_len)
    tk = min(tk, seq_len)
    assert seq_len % tq == 0 and seq_len % tk == 0
    nq, nk = seq_len // tq, seq_len // tk
    kv_blk = lambda qi, ki: jnp.minimum(ki, ((qi + 1) * tq - 1) // tk)
    stat = pltpu.VMEM((tq, 1), F32)
    acc = pltpu.VMEM((tq, B_VDIM), F32)
    qsc = pltpu.VMEM((tq, B_VDIM), BF16)
    return pl.pallas_call(
        functools.partial(_attn_prompt_body, tq=tq, tk=tk, lam_init=lam_init),
        grid=(bsz, B_HEADS, nq, nk),
        in_specs=[
            pl.BlockSpec((1, tq, B_VDIM), lambda b, h, qi, ki: (b, qi, h)),
            pl.BlockSpec((1, tk, B_VDIM), lambda b, h, qi, ki: (b, kv_blk(qi, ki), B_HEADS + h)),
            pl.BlockSpec((1, tk, B_VDIM), lambda b, h, qi, ki: (b, kv_blk(qi, ki), 2 * B_HEADS + h)),
            pl.BlockSpec((4, B_QK), lambda b, h, qi, ki: (0, 0)),
            pl.BlockSpec((1, B_VDIM), lambda b, h, qi, ki: (0, 0)),
        ],
        out_specs=pl.BlockSpec((1, tq, B_VDIM), lambda b, h, qi, ki: (b, qi, h)),
        out_shape=jax.ShapeDtypeStruct((bsz, seq_len, B_WIDTH), BF16),
        scratch_shapes=[qsc, qsc, stat, stat, acc, stat, stat, acc],
        compiler_params=_cparams(("parallel", "parallel", "parallel", "arbitrary")),
        name="attn_prompt",
    )(qk, qk, ub, lqk, subln.reshape(1, B_VDIM))


def _attn_decode_body(pt_ref, q_ref, kn_ref, vn_ref, ck_ref, cv_ref, lqk_ref, sub_ref, o_ref,
                      q8_ref, m_ref, l_ref, acc_ref, *, nq, n_pages, lam_init):
    del pt_ref
    p = pl.program_id(1)
    rows = 2 * nq

    @pl.when(p == 0)
    def _():
        q = q_ref[0] * (B_QK ** -0.5)
        lane = lax.broadcasted_iota(jnp.int32, (nq, B_VDIM), 1)
        for h in range(B_HEADS):
            qh = q[:, h * B_VDIM:(h + 1) * B_VDIM]
            q8_ref[h] = jnp.concatenate([jnp.where(lane < B_QK, qh, 0.0), jnp.where(lane >= B_QK, qh, 0.0)], axis=0)
        m_ref[...] = jnp.full(m_ref.shape, NEG_INF, F32)
        l_ref[...] = jnp.zeros(l_ref.shape, F32)
        acc_ref[...] = jnp.zeros(acc_ref.shape, F32)

    @pl.when(p < n_pages)
    def _():
        for h in range(B_HEADS):
            hs = slice(h * B_VDIM, (h + 1) * B_VDIM)
            kh = ck_ref[0, 0, :, hs].astype(BF16)
            vh = cv_ref[0, 0, :, hs].astype(BF16)
            s = lax.dot_general(q8_ref[h].astype(BF16), kh, _NT, preferred_element_type=F32)
            m_prev = m_ref[h]
            m_new = jnp.maximum(m_prev, jnp.max(s, axis=-1, keepdims=True))
            alpha = jnp.exp(m_prev - m_new)
            pr = jnp.exp(s - m_new)
            l_ref[h] = alpha * l_ref[h] + jnp.sum(pr, axis=-1, keepdims=True)
            acc_ref[h] = alpha * acc_ref[h] + jnp.dot(pr.astype(BF16), vh, preferred_element_type=F32)
            m_ref[h] = m_new

    @pl.when(p == n_pages)
    def _():
        lam = _lam(lqk_ref, lam_init)
        qrow = lax.broadcasted_iota(jnp.int32, (rows, 1), 0) % nq
        kn = kn_ref[0]
        vn = vn_ref[0]
        for h in range(B_HEADS):
            hs = slice(h * B_VDIM, (h + 1) * B_VDIM)
            q8 = q8_ref[h].astype(BF16).astype(F32)
            knh = kn[:, hs].astype(BF16).astype(F32)
            vnh = vn[:, hs].astype(BF16).astype(F32)
            ss = [jnp.where(qrow >= j, jnp.sum(q8 * knh[j:j + 1, :], axis=-1, keepdims=True), NEG_INF)
                  for j in range(nq)]
            m_prev = m_ref[h]
            m_new = m_prev
            for sj in ss:
                m_new = jnp.maximum(m_new, sj)
            alpha = jnp.exp(m_prev - m_new)
            l = alpha * l_ref[h]
            acc = alpha * acc_ref[h]
            for j, sj in enumerate(ss):
                pj = jnp.exp(sj - m_new)
                l = l + pj
                acc = acc + pj.astype(BF16).astype(F32) * vnh[j:j + 1, :]
            y = _diff_finish(acc[0:nq], l[0:nq], acc[nq:rows], l[nq:rows], lam, sub_ref[...], lam_init)
            o_ref[0, :, hs] = y.astype(o_ref.dtype)


def attn_decode(qk, ub, cache_k, cache_v, page_table, layer, lqk, subln, lam_init):
    bsz, nq, _ = qk.shape
    n_pages = page_table.shape[1]
    page = cache_k.shape[2]
    rows = 2 * nq
    cache_spec = pl.BlockSpec(
        (1, 1, page, B_WIDTH),
        lambda b, p, pt: (layer, pt[b * n_pages + jnp.minimum(p, n_pages - 1)], 0, 0))
    grid_spec = pltpu.PrefetchScalarGridSpec(
        num_scalar_prefetch=1,
        grid=(bsz, n_pages + 1),
        in_specs=[
            pl.BlockSpec((1, nq, B_WIDTH), lambda b, p, pt: (b, 0, 0)),
            pl.BlockSpec((1, nq, B_WIDTH), lambda b, p, pt: (b, 0, 1)),
            pl.BlockSpec((1, nq, B_WIDTH), lambda b, p, pt: (b, 0, 2)),
            cache_spec, cache_spec,
            pl.BlockSpec((4, B_QK), lambda b, p, pt: (0, 0)),
            pl.BlockSpec((1, B_VDIM), lambda b, p, pt: (0, 0)),
        ],
        out_specs=pl.BlockSpec((1, nq, B_WIDTH), lambda b, p, pt: (b, 0, 0)),
        scratch_shapes=[
            pltpu.VMEM((B_HEADS, rows, B_VDIM), F32),
            pltpu.VMEM((B_HEADS, rows, 1), F32),
            pltpu.VMEM((B_HEADS, rows, 1), F32),
            pltpu.VMEM((B_HEADS, rows, B_VDIM), F32),
        ],
    )
    return pl.pallas_call(
        functools.partial(_attn_decode_body, nq=nq, n_pages=n_pages, lam_init=lam_init),
        grid_spec=grid_spec,
        out_shape=jax.ShapeDtypeStruct((bsz, nq, B_WIDTH), F32),
        compiler_params=_cparams(("parallel", "arbitrary")),
        name="attn_decode",
    )(page_table.reshape(-1), qk, qk, ub, cache_k, cache_v, lqk, subln.reshape(1, B_VDIM)).astype(BF16)


_TN = (((0,), (0,)), ((), ()))


def _hgrn_body(q_ref, f_ref, i_ref, g_ref, lb_ref, cn_ref, s0_ref, o_ref, sfin_ref, s_ref, *, tc, n_valid):
    c = pl.program_id(2)

    @pl.when(c == 0)
    def _():
        s_ref[...] = s0_ref[0, 0]

    nrow = -(-tc // C_CHUNK) * C_CHUNK
    pad = lambda x: x if nrow == tc else jnp.concatenate([x, jnp.zeros((nrow - tc, x.shape[1]), F32)], axis=0)
    q = _silu(pad(q_ref[0]))
    lb = lb_ref[...]
    fs = lb + (1.0 - lb) * _sigmoid(pad(f_ref[0]))
    kin = 1.0 - fs
    gl = jnp.log(fs)
    iv = pad(i_ref[0])
    row = lax.broadcasted_iota(jnp.int32, (nrow, C_DK), 0)
    if n_valid < nrow:
        kin = jnp.where(row < n_valid, kin, 0.0)
        gl = jnp.where(row < n_valid, gl, 0.0)
    b = gl
    sh = 1
    while sh < C_CHUNK:
        b = b + jnp.where(row % C_CHUNK >= sh, pltpu.roll(b, sh, 0), 0.0)
        sh *= 2
    trow = lax.broadcasted_iota(jnp.int32, (C_CHUNK, C_DK), 0)
    eye = (lax.broadcasted_iota(jnp.int32, (C_DK, C_DK), 0)
           == lax.broadcasted_iota(jnp.int32, (C_DK, C_DK), 1)).astype(F32)
    outs = []
    for ch in range(nrow // C_CHUNK):
        rs = slice(ch * C_CHUNK, (ch + 1) * C_CHUNK)
        bc, qc, kc, ic = b[rs], q[rs], kin[rs], iv[rs]
        s = s_ref[...]
        o = jnp.dot((qc * jnp.exp(bc)).astype(BF16), s.astype(BF16), preferred_element_type=F32)
        for sidx in range(C_CHUNK):
            dec = jnp.exp(jnp.where(trow >= sidx, bc - bc[sidx:sidx + 1, :], NEG_INF))
            att = jnp.sum(qc * kc[sidx:sidx + 1, :] * dec, axis=-1, keepdims=True)
            o = o + att * ic[sidx:sidx + 1, :]
        bl = bc[C_CHUNK - 1:C_CHUNK, :]
        kd = kc * jnp.exp(bl - bc)
        inc = lax.dot_general(kd.astype(BF16), ic.astype(BF16), _TN, preferred_element_type=F32)
        ecol = jnp.sum(eye * jnp.exp(bl), axis=1, keepdims=True)
        s_ref[...] = s * ecol + inc
        outs.append(o)
    o = jnp.concatenate(outs, axis=0)[0:tc]
    y = o * lax.rsqrt(jnp.mean(o * o, axis=-1, keepdims=True) + EPS) * cn_ref[...]
    o_ref[0] = (y * _silu(g_ref[0])).astype(o_ref.dtype)

    @pl.when(c == pl.num_programs(2) - 1)
    def _():
        sfin_ref[0, 0] = s_ref[...]


def hgrn(uc, s0, lb, c_norm, tc=128):
    bsz, seq_len, _ = uc.shape
    tc = min(tc, seq_len)
    assert seq_len % tc == 0
    tok = lambda off: pl.BlockSpec((1, tc, LANES), lambda b, h, c: (b, c, off + h))
    st = pl.BlockSpec((1, 1, C_DK, C_DV), lambda b, h, c: (b, h, 0, 0))
    return pl.pallas_call(
        functools.partial(_hgrn_body, tc=tc, n_valid=tc),
        grid=(bsz, C_HEADS, seq_len // tc),
        in_specs=[tok(0), tok(C_HEADS), tok(2 * C_HEADS), tok(3 * C_HEADS),
                  pl.BlockSpec((1, LANES), lambda b, h, c: (0, h)),
                  pl.BlockSpec((1, C_DV), lambda b, h, c: (0, 0)),
                  st],
        out_specs=[pl.BlockSpec((1, tc, LANES), lambda b, h, c: (b, c, h)), st],
        out_shape=[jax.ShapeDtypeStruct((bsz, seq_len, C_WIDTH), BF16),
                   jax.ShapeDtypeStruct((bsz, C_HEADS, C_DK, C_DV), F32)],
        scratch_shapes=[pltpu.VMEM((C_DK, C_DV), F32)],
        compiler_params=_cparams(("parallel", "parallel", "arbitrary")),
        name="hgrn",
    )(uc, uc, uc, uc, lb.reshape(1, C_FK), c_norm.reshape(1, C_DV), s0)


def _pair_state(s):
    bsz = s.shape[0]
    return s.reshape(bsz, A_PAIRS, 2, A_HEAD, A_HEAD).transpose(0, 1, 3, 2, 4).reshape(bsz, A_PAIRS, A_HEAD, LANES)


def _unpair_state(s):
    bsz = s.shape[0]
    return s.reshape(bsz, A_PAIRS, A_HEAD, 2, A_HEAD).transpose(0, 1, 3, 2, 4).reshape(bsz, A_HEADS, A_HEAD, A_HEAD)


def _layer_params(l, P):
    w_in = P["w_in"][l]
    b0 = A_COLS
    c0 = A_COLS + 3 * B_WIDTH
    padc = A_COLS_PAD - A_COLS
    w2 = jnp.zeros((LANES, 2 * A_WIDTH), F32)
    w2 = w2.at[0:A_DECAY_R, 0:A_WIDTH].set(P["a_w_up"][l])
    w2 = w2.at[A_DECAY_R:A_DECAY_R + A_ICLR_R, A_WIDTH:].set(P["a_a_up"][l])
    return dict(
        w_a=jnp.pad(w_in[:, :b0], ((0, 0), (0, padc))),
        w_b=w_in[:, b0:c0],
        w_c=w_in[:, c0:],
        mu=jnp.pad(P["a_mu"][l], (0, padc)),
        w2=w2,
        gup=jnp.pad(P["a_g_up"][l], ((0, 2 * LANES - A_GATE_R), (0, 0))),
        lqk=jnp.stack([P["b_lq1"][l], P["b_lk1"][l], P["b_lq2"][l], P["b_lk2"][l]]),
        r_k=P["a_r_k"][l].reshape(A_WIDTH),
    )


def _run_trunk(x, pos, caches, wkv0, shift0, hgrn0, conv0, lbs, P, LP):
    bsz, seq_len, d = x.shape
    t = bsz * seq_len
    h = x.reshape(t, d)
    tables = rope_tables(pos)
    padc = A_COLS_PAD - A_COLS
    ks, vs, wkvs, shifts, hgrns, convs = [], [], [], [], [], []
    for l in range(DEPTH):
        lp = LP[l]
        hn = rmsnorm_rows(h, P["norm_mix"][l], BF16)
        ua = matmul(hn, lp["w_a"]).reshape(bsz, seq_len, A_COLS_PAD)
        ub = matmul(hn, lp["w_b"]).reshape(bsz, seq_len, 3 * B_WIDTH)
        uc = matmul(hn, lp["w_c"]).reshape(bsz, seq_len, 4 * C_WIDTH)
        r, dcy, kp, v, kk, bb, g, bonus = rwkv_pre(
            ua, jnp.pad(shift0[l], ((0, 0), (0, padc))), lp["mu"], lp["w2"], lp["gup"],
            P["a_w0"][l], P["a_a0"][l], P["a_k_k"][l], P["a_k_a"][l], lp["r_k"])
        o_a, s_a = rwkv_scan(r, dcy, kp, v, kk, bb, _pair_state(wkv0[l]))
        flat = lambda z: z.reshape(t, A_WIDTH)
        oa = rwkv_post(flat(o_a), flat(bonus), flat(g), P["a_ln_w"][l], P["a_ln_b"][l])
        lam_init = 0.8 - 0.6 * math.exp(-0.3 * l)
        qk = rope_qk(ub, tables)
        if caches is None:
            ob = attn_prompt(qk, ub, lp["lqk"], P["b_subln"][l], lam_init)
        else:
            ob = attn_decode(qk, ub, caches[0], caches[1], caches[2], l, lp["lqk"], P["b_subln"][l], lam_init)
        oc, s_c = hgrn(uc, hgrn0[l], lbs[l], P["c_norm"][l])
        mix = jnp.concatenate([oa, ob.reshape(t, B_WIDTH), oc.reshape(t, C_WIDTH)], axis=-1)
        h = matmul(mix, P["w_out"][l], res=h)
        hf = rmsnorm_rows(h, P["norm_ffn"][l], BF16)
        mid, buf = ffn_up(hf, P["f_gate"][l], P["f_up"][l], P["f_conv"][l], P["f_conv_b"][l], conv0[l], seq_len)
        h = matmul(mid, P["f_down"][l], res=h, tn=512, tk=1024)
        ks.append(qk[..., B_WIDTH:].reshape(bsz, seq_len, B_HEADS, B_VDIM))
        vs.append(ub[..., 2 * B_WIDTH:].reshape(bsz, seq_len, B_HEADS, B_VDIM))
        wkvs.append(_unpair_state(s_a))
        shifts.append(ua[:, -1, :A_COLS])
        hgrns.append(s_c)
        convs.append(buf)
    y = rmsnorm_rows(h, P["norm_final"], F32).reshape(bsz, seq_len, d)
    st = jnp.stack
    return y, st(ks), st(vs), st(wkvs), st(shifts), st(hgrns), st(convs)


def kernel(x_prompt, x_sample, cache_k, cache_v, page_table, state_wkv, state_shift, state_hgrn, state_conv,
           norm_mix, w_in, a_mu, a_w0, a_w_up, a_a0, a_a_up, a_g_up, a_k_k, a_k_a, a_r_k, a_ln_w, a_ln_b,
           b_lq1, b_lk1, b_lq2, b_lk2, b_subln, c_lb_logits, c_norm, w_out, norm_ffn, f_gate, f_up,
           f_conv, f_conv_b, f_down, norm_final):
    P = dict(norm_mix=norm_mix, w_in=w_in, a_mu=a_mu, a_w0=a_w0, a_w_up=a_w_up, a_a0=a_a0, a_a_up=a_a_up,
             a_g_up=a_g_up, a_k_k=a_k_k, a_k_a=a_k_a, a_r_k=a_r_k, a_ln_w=a_ln_w, a_ln_b=a_ln_b,
             b_lq1=b_lq1, b_lk1=b_lk1, b_lq2=b_lq2, b_lk2=b_lk2, b_subln=b_subln, c_norm=c_norm,
             w_out=w_out, norm_ffn=norm_ffn, f_gate=f_gate, f_up=f_up, f_conv=f_conv, f_conv_b=f_conv_b,
             f_down=f_down, norm_final=norm_final)
    LP = [_layer_params(l, P) for l in range(DEPTH)]
    lbs = lower_bounds(c_lb_logits)
    bp, lp_ = x_prompt.shape[:2]
    n_pool, page = cache_k.shape[1], cache_k.shape[2]
    past_len = page_table.shape[1] * page
    pos_p = jnp.arange(lp_)
    pos_s = past_len + jnp.arange(x_sample.shape[1])
    zeros = lambda *s: jnp.zeros(s, F32)
    out_p = _run_trunk(
        x_prompt, pos_p, None,
        zeros(DEPTH, bp, A_HEADS, A_HEAD, A_HEAD), zeros(DEPTH, bp, A_COLS),
        zeros(DEPTH, bp, C_HEADS, C_DK, C_DV), zeros(DEPTH, bp, CONV_W - 1, D_FF), lbs, P, LP)
    caches = (cache_k.reshape(DEPTH, n_pool, page, B_WIDTH), cache_v.reshape(DEPTH, n_pool, page, B_WIDTH), page_table)
    out_s = _run_trunk(x_sample, pos_s, caches, state_wkv, state_shift, state_hgrn, state_conv, lbs, P, LP)
    y_p, k_p, v_p, wkv_p, sh_p, hg_p, cv_p = out_p
    y_s, k_s, v_s, wkv_s, sh_s, hg_s, cv_s = out_s
    return (y_p, y_s, k_p, v_p, k_s, v_s, wkv_p, wkv_s, sh_p, sh_s, hg_p, hg_s, cv_p, cv_s)
```

```python
import functools
import math

import jax
import jax.numpy as jnp
from jax import lax
from jax.experimental import pallas as pl
from jax.experimental.pallas import tpu as pltpu

F32 = jnp.float32
BF16 = jnp.bfloat16

V7X_VMEM_BYTES = 64 * 1024 * 1024
VMEM_LIMIT = V7X_VMEM_BYTES - 8 * 1024 * 1024
LANES = 128
SUBLANES = 8

D_MODEL = 4096
DEPTH = 4
A_WIDTH = D_MODEL // 4
A_HEAD = 64
A_HEADS = A_WIDTH // A_HEAD
A_PAIRS = A_HEADS // 2
A_DECAY_R = 64
A_ICLR_R = 64
A_GATE_R = 160
A_COLS = 3 * A_WIDTH + A_DECAY_R + A_ICLR_R + A_GATE_R
A_COLS_PAD = 3 * A_WIDTH + 3 * LANES
A_GN_EPS = 64e-5
B_WIDTH = D_MODEL // 2
B_HEADS = 16
B_VDIM = B_WIDTH // B_HEADS
B_QK = B_VDIM // 2
B_ROT = B_QK // 4
ROPE_THETA = 500000.0
C_WIDTH = D_MODEL - A_WIDTH - B_WIDTH
C_HEADS = 8
C_DK = 128
C_DV = C_WIDTH // C_HEADS
C_FK = C_HEADS * C_DK
C_CHUNK = 16
D_FF = 11008
CONV_W = 3
EPS = 1e-5
NEG_INF = float("-inf")


def _cparams(sem):
    return pltpu.CompilerParams(dimension_semantics=sem, vmem_limit_bytes=VMEM_LIMIT)


def _sigmoid(x):
    return 1.0 / (1.0 + jnp.exp(-x))


def _silu(x):
    return x * _sigmoid(x)


def _block_ones(dtype):
    r = lax.broadcasted_iota(jnp.int32, (LANES, LANES), 0) // A_HEAD
    c = lax.broadcasted_iota(jnp.int32, (LANES, LANES), 1) // A_HEAD
    return (r == c).astype(dtype)


def _rmsnorm_body(x_ref, w_ref, o_ref):
    x = x_ref[...]
    y = x * lax.rsqrt(jnp.mean(x * x, axis=-1, keepdims=True) + EPS)
    o_ref[...] = (y * w_ref[...]).astype(o_ref.dtype)


def rmsnorm_rows(x, w, out_dtype, tr=256):
    t, d = x.shape
    tr = min(tr, t)
    return pl.pallas_call(
        _rmsnorm_body,
        grid=(pl.cdiv(t, tr),),
        in_specs=[pl.BlockSpec((tr, d), lambda i: (i, 0)), pl.BlockSpec((1, d), lambda i: (0, 0))],
        out_specs=pl.BlockSpec((tr, d), lambda i: (i, 0)),
        out_shape=jax.ShapeDtypeStruct((t, d), out_dtype),
        compiler_params=_cparams(("parallel",)),
        name="rmsnorm_rows",
    )(x, w.reshape(1, d))


def _mm_body(*refs, nk, tk, k_total, has_res):
    if has_res:
        a_ref, w_ref, r_ref, o_ref = refs
    else:
        a_ref, w_ref, o_ref = refs
        r_ref = None
    a = a_ref[...]
    w = w_ref[...].astype(BF16)
    if nk == 1:
        acc = jnp.dot(a, w, preferred_element_type=F32)
        if has_res:
            acc = acc + r_ref[...]
        o_ref[...] = acc.astype(o_ref.dtype)
        return
    k = pl.program_id(2)
    if k_total % tk:
        valid = k_total - k * tk
        a = jnp.where(lax.broadcasted_iota(jnp.int32, a.shape, 1) < valid, a, jnp.zeros_like(a))
        w = jnp.where(lax.broadcasted_iota(jnp.int32, w.shape, 0) < valid, w, jnp.zeros_like(w))
    part = jnp.dot(a, w, preferred_element_type=F32)

    @pl.when(k == 0)
    def _():
        o_ref[...] = (part + r_ref[...]) if has_res else part

    @pl.when(k > 0)
    def _():
        o_ref[...] += part


def matmul(a, w, res=None, out_dtype=F32, tm=2048, tn=256, tk=None):
    m, kdim = a.shape
    n = w.shape[1]
    tm = min(tm, m)
    tk = kdim if tk is None else tk
    nk = pl.cdiv(kdim, tk)
    if nk > 1:
        assert out_dtype == F32
    grid = (pl.cdiv(m, tm), pl.cdiv(n, tn), nk)
    a_kw = {}
    if nk == 1 and grid[1] > 1:
        a_kw = dict(pipeline_mode=pl.Buffered(1))
    in_specs = [pl.BlockSpec((tm, tk), lambda i, j, k: (i, k), **a_kw),
                pl.BlockSpec((tk, tn), lambda i, j, k: (k, j))]
    args = [a, w]
    if res is not None:
        in_specs.append(pl.BlockSpec((tm, tn), lambda i, j, k: (i, j)))
        args.append(res)
    return pl.pallas_call(
        functools.partial(_mm_body, nk=nk, tk=tk, k_total=kdim, has_res=res is not None),
        grid=grid,
        in_specs=in_specs,
        out_specs=pl.BlockSpec((tm, tn), lambda i, j, k: (i, j)),
        out_shape=jax.ShapeDtypeStruct((m, n), out_dtype),
        compiler_params=_cparams(("parallel", "parallel", "arbitrary")),
        name="matmul",
    )(*args)


def _ffn_up_body(a_ref, wg_ref, wu_ref, cw_ref, cb_ref, buf_ref, o_ref, nbuf_ref, ext_ref, *, nb, seq_len):
    tm = nb * seq_len
    a = a_ref[...]
    gate = jnp.dot(a, wg_ref[...].astype(BF16), preferred_element_type=F32)
    up = jnp.dot(a, wu_ref[...].astype(BF16), preferred_element_type=F32)
    ext_ref[0:SUBLANES, :] = jnp.zeros((SUBLANES, gate.shape[1]), F32)
    ext_ref[SUBLANES:SUBLANES + tm, :] = gate
    g1 = ext_ref[SUBLANES - 1:SUBLANES - 1 + tm, :]
    g2 = ext_ref[SUBLANES - 2:SUBLANES - 2 + tm, :]
    row = lax.broadcasted_iota(jnp.int32, gate.shape, 0)
    for s in range(nb):
        b0 = buf_ref[s, 0:1, :]
        b1 = buf_ref[s, 1:2, :]
        g1 = jnp.where(row == s * seq_len, b1, g1)
        g2 = jnp.where(row == s * seq_len, b0, g2)
        g2 = jnp.where(row == s * seq_len + 1, b1, g2)
        last = SUBLANES + (s + 1) * seq_len
        nbuf_ref[s, :, :] = ext_ref[last - 2:last, :]
    conv = cb_ref[...] + g2 * cw_ref[0:1, :] + g1 * cw_ref[1:2, :] + gate * cw_ref[2:3, :]
    o_ref[...] = (_silu(conv) * up).astype(o_ref.dtype)


def ffn_up(hn, w_gate, w_up, w_conv, b_conv, buf, seq_len, tn=256):
    t, d = hn.shape
    n = w_gate.shape[1]
    nseq = buf.shape[0]
    nb = 1 if seq_len >= 256 else nseq
    tm = nb * seq_len
    assert seq_len >= 2 and t == nseq * seq_len and n % tn == 0
    grid = (t // tm, n // tn)
    return pl.pallas_call(
        functools.partial(_ffn_up_body, nb=nb, seq_len=seq_len),
        grid=grid,
        in_specs=[
            pl.BlockSpec((tm, d), lambda i, j: (i, 0), pipeline_mode=pl.Buffered(1)),
            pl.BlockSpec((d, tn), lambda i, j: (0, j)),
            pl.BlockSpec((d, tn), lambda i, j: (0, j)),
            pl.BlockSpec((CONV_W, tn), lambda i, j: (0, j)),
            pl.BlockSpec((1, tn), lambda i, j: (0, j)),
            pl.BlockSpec((nb, CONV_W - 1, tn), lambda i, j: (i, 0, j)),
        ],
        out_specs=[
            pl.BlockSpec((tm, tn), lambda i, j: (i, j)),
            pl.BlockSpec((nb, CONV_W - 1, tn), lambda i, j: (i, 0, j)),
        ],
        out_shape=[
            jax.ShapeDtypeStruct((t, n), BF16),
            jax.ShapeDtypeStruct((nseq, CONV_W - 1, n), F32),
        ],
        scratch_shapes=[pltpu.VMEM((SUBLANES + tm, tn), F32)],
        compiler_params=_cparams(("parallel", "parallel")),
        name="ffn_up",
    )(hn, w_gate, w_up, w_conv, b_conv.reshape(1, n), buf)


def _lower_bounds_body(x_ref, o_ref):
    x = x_ref[...]
    e = jnp.exp(x - jnp.max(x, axis=0, keepdims=True))
    p = e / jnp.sum(e, axis=0, keepdims=True)
    run = jnp.zeros_like(p[0:1, :])
    for l in range(x.shape[0]):
        run = run + p[l:l + 1, :]
        o_ref[l:l + 1, :] = run - p[0:1, :]


def lower_bounds(logits):
    return pl.pallas_call(
        _lower_bounds_body,
        out_shape=jax.ShapeDtypeStruct(logits.shape, F32),
        name="lower_bounds",
    )(logits)


def _seg_sum(x, ones):
    parts = []
    for s in range(x.shape[1] // LANES):
        parts.append(jnp.dot(x[:, s * LANES:(s + 1) * LANES], ones,
                             preferred_element_type=F32, precision=lax.Precision.HIGHEST))
    return jnp.concatenate(parts, axis=1)


def _rwkv_pre_body(u_ref, prev_ref, mu_ref, w2_ref, gup_ref, w0_ref, a0_ref, kk_ref, ka_ref, rk_ref,
                   r_o, d_o, k_o, v_o, kk_o, b_o, g_o, bonus_o, ext_ref, *, tm):
    j = pl.program_id(1)
    u = u_ref[0]

    @pl.when(j == 0)
    def _():
        ext_ref[SUBLANES - 1:SUBLANES, :] = prev_ref[0]

    ext_ref[SUBLANES:SUBLANES + tm, :] = u
    shifted = ext_ref[SUBLANES - 1:SUBLANES - 1 + tm, :]
    ext_ref[SUBLANES - 1:SUBLANES, :] = u[tm - 1:tm, :]
    xm = u + (shifted - u) * mu_ref[...]
    w = A_WIDTH
    r = xm[:, 0:w]
    k = xm[:, w:2 * w]
    v = xm[:, 2 * w:3 * w]
    lr = xm[:, 3 * w:3 * w + LANES]
    gd = xm[:, 3 * w + LANES:3 * w + 3 * LANES]
    lane = lax.broadcasted_iota(jnp.int32, lr.shape, 1)
    lr = jnp.where(lane < A_DECAY_R, jnp.tanh(lr), lr)
    wa = jnp.dot(lr.astype(BF16), w2_ref[...].astype(BF16), preferred_element_type=F32)
    wraw = -(w0_ref[...] + wa[:, 0:w])
    softplus = jnp.maximum(wraw, 0.0) + jnp.log(1.0 + jnp.exp(-jnp.abs(wraw)))
    decay = jnp.exp(-jnp.exp(-softplus - 0.5))
    a = _sigmoid(a0_ref[...] + wa[:, w:2 * w])
    g = jnp.dot(_sigmoid(gd).astype(BF16), gup_ref[...].astype(BF16), preferred_element_type=F32)
    ones = _block_ones(F32)
    kk = k * kk_ref[...]
    norm = jnp.sqrt(_seg_sum(kk * kk, ones))
    kk = kk / jnp.maximum(norm, 1e-12)
    kp = k * (1.0 + (a - 1.0) * ka_ref[...])
    bonus = _seg_sum(r * kp * rk_ref[...], ones) * v
    r_o[0] = r
    d_o[0] = decay
    k_o[0] = kp
    v_o[0] = v
    kk_o[0] = kk
    b_o[0] = kk * a
    g_o[0] = g
    bonus_o[0] = bonus


def rwkv_pre(ua, prev, mu, w2, gup, w0, a0, k_k, k_a, r_k, tm=256):
    bsz, seq_len, cols = ua.shape
    tm = min(tm, seq_len)
    assert seq_len % tm == 0 and cols == A_COLS_PAD
    w = A_WIDTH
    row = lambda x: x.reshape(1, -1)
    vec = lambda n: pl.BlockSpec((1, n), lambda b, j: (0, 0))
    out_spec = pl.BlockSpec((1, tm, w), lambda b, j: (b, j, 0))
    out_sds = jax.ShapeDtypeStruct((bsz, seq_len, w), F32)
    return pl.pallas_call(
        functools.partial(_rwkv_pre_body, tm=tm),
        grid=(bsz, seq_len // tm),
        in_specs=[
            pl.BlockSpec((1, tm, cols), lambda b, j: (b, j, 0)),
            pl.BlockSpec((1, 1, cols), lambda b, j: (b, 0, 0)),
            vec(cols),
            pl.BlockSpec((LANES, 2 * w), lambda b, j: (0, 0)),
            pl.BlockSpec((2 * LANES, w), lambda b, j: (0, 0)),
            vec(w), vec(w), vec(w), vec(w), vec(w),
        ],
        out_specs=[out_spec] * 8,
        out_shape=[out_sds] * 8,
        scratch_shapes=[pltpu.VMEM((SUBLANES + tm, cols), F32)],
        compiler_params=_cparams(("parallel", "arbitrary")),
        name="rwkv_pre",
    )(ua, prev.reshape(bsz, 1, cols), row(mu), w2, gup, row(w0), row(a0), row(k_k), row(k_a), row(r_k))


def _rwkv_scan_body(r_ref, d_ref, k_ref, v_ref, kk_ref, b_ref, s0_ref, o_ref, sfin_ref, s_ref, p2_ref, *, nbt, tc):
    c = pl.program_id(1)

    @pl.when(c == 0)
    def _():
        s_ref[...] = s0_ref[...]

    sub = min(SUBLANES, tc)
    npair = nbt * A_PAIRS
    ones = _block_ones(BF16)
    ones2 = jnp.concatenate([ones, ones], axis=0)
    eye2 = (lax.broadcasted_iota(jnp.int32, (A_HEAD, LANES), 0)
            == lax.broadcasted_iota(jnp.int32, (A_HEAD, LANES), 1) % A_HEAD)
    eye2f = eye2.astype(F32)
    wr = lax.broadcasted_iota(jnp.int32, (2 * sub, sub * LANES), 0)
    wc = lax.broadcasted_iota(jnp.int32, (2 * sub, sub * LANES), 1)
    wsel = jnp.where((wc // LANES == wr % sub) & ((wc % LANES) // A_HEAD == wr // sub), 1.0, 0.0).astype(BF16)
    bc = lambda x: jnp.broadcast_to(x, (A_HEAD, LANES))
    prs = [divmod(q, A_PAIRS) for q in range(npair)]

    def group(gi, carry):
        rows = pl.ds(pl.multiple_of(gi * sub, sub), sub)
        tile = lambda ref, q: ref[prs[q][0], rows, pl.ds(prs[q][1] * LANES, LANES)]
        for j in range(sub):
            row = lambda x8: x8[j:j + 1, :]
            p1s, p3s = [], []
            for q, (nb, p) in enumerate(prs):
                v8 = tile(v_ref, q)
                v_hi8 = v8.astype(BF16).astype(F32)
                p1s.append((s_ref[nb, p] * bc(row(tile(kk_ref, q)))).astype(BF16))
                p3s.append(jnp.concatenate([(eye2f * bc(row(v_hi8))).astype(BF16),
                                            (eye2f * bc(row(v8 - v_hi8))).astype(BF16)], axis=1))
            sa_all = jnp.dot(jnp.concatenate(p1s, axis=0), ones, preferred_element_type=F32)
            vc_all = jnp.dot(jnp.concatenate(p3s, axis=0), ones2, preferred_element_type=F32)
            for q, (nb, p) in enumerate(prs):
                qs = slice(q * A_HEAD, (q + 1) * A_HEAD)
                s = (s_ref[nb, p] * bc(row(tile(d_ref, q))) - sa_all[qs] * bc(row(tile(b_ref, q)))
                     + vc_all[qs] * bc(row(tile(k_ref, q))))
                s_ref[nb, p] = s
                p2_ref[qs, j * LANES:(j + 1) * LANES] = (s * bc(row(tile(r_ref, q)))).astype(BF16)
        for q, (nb, p) in enumerate(prs):
            qs = slice(q * A_HEAD, (q + 1) * A_HEAD)
            ot = lax.dot_general(wsel, p2_ref[qs, :], _NT, preferred_element_type=F32)
            o_ref[nb, rows, pl.ds(p * LANES, LANES)] = jnp.concatenate([ot[0:sub], ot[sub:2 * sub]], axis=1)
        return carry

    lax.fori_loop(0, tc // sub, group, 0)

    @pl.when(c == pl.num_programs(1) - 1)
    def _():
        sfin_ref[...] = s_ref[...]


def rwkv_scan(r, d, k, v, kk, b, s0, tc=64, nbt=2):
    bsz, seq_len, w = r.shape
    tc = min(tc, seq_len)
    nbt = min(nbt, bsz)
    assert seq_len % tc == 0 and bsz % nbt == 0 and w == A_WIDTH
    tok = pl.BlockSpec((nbt, tc, w), lambda i, c: (i, c, 0))
    st = pl.BlockSpec((nbt, A_PAIRS, A_HEAD, LANES), lambda i, c: (i, 0, 0, 0))
    return pl.pallas_call(
        functools.partial(_rwkv_scan_body, nbt=nbt, tc=tc),
        grid=(bsz // nbt, seq_len // tc),
        in_specs=[tok] * 6 + [st],
        out_specs=[tok, st],
        out_shape=[jax.ShapeDtypeStruct((bsz, seq_len, w), F32),
                   jax.ShapeDtypeStruct((bsz, A_PAIRS, A_HEAD, LANES), F32)],
        scratch_shapes=[pltpu.VMEM((nbt, A_PAIRS, A_HEAD, LANES), F32),
                        pltpu.VMEM((nbt * A_PAIRS * A_HEAD, min(SUBLANES, tc) * LANES), BF16)],
        compiler_params=_cparams(("parallel", "arbitrary")),
        name="rwkv_scan",
    )(r, d, k, v, kk, b, s0)


def _rwkv_post_body(o_ref, bonus_ref, g_ref, lnw_ref, lnb_ref, out_ref):
    o = o_ref[...]
    ones = _block_ones(F32)
    inv_n = 1.0 / A_HEAD
    mean = _seg_sum(o, ones) * inv_n
    cen = o - mean
    var = _seg_sum(cen * cen, ones) * inv_n
    y = cen * lax.rsqrt(var + A_GN_EPS) * lnw_ref[...] + lnb_ref[...]
    out_ref[...] = ((y + bonus_ref[...]) * g_ref[...]).astype(out_ref.dtype)


def rwkv_post(o, bonus, g, ln_w, ln_b, tr=256):
    t, w = o.shape
    tr = min(tr, t)
    tok = pl.BlockSpec((tr, w), lambda i: (i, 0))
    vec = pl.BlockSpec((1, w), lambda i: (0, 0))
    return pl.pallas_call(
        _rwkv_post_body,
        grid=(pl.cdiv(t, tr),),
        in_specs=[tok, tok, tok, vec, vec],
        out_specs=tok,
        out_shape=jax.ShapeDtypeStruct((t, w), BF16),
        compiler_params=_cparams(("parallel",)),
        name="rwkv_post",
    )(o, bonus, g, ln_w.reshape(1, w), ln_b.reshape(1, w))


def _rope_body(x_ref, cos_ref, sa_ref, sb_ref, o_ref, *, nslab):
    cos = cos_ref[...]
    sin_a = sa_ref[...]
    sin_b = sb_ref[...]
    half = B_ROT // 2
    for s in range(nslab):
        x = x_ref[0, :, s * LANES:(s + 1) * LANES]
        y = x * cos + pltpu.roll(x, half, 1) * sin_a + pltpu.roll(x, LANES - half, 1) * sin_b
        o_ref[0, :, s * LANES:(s + 1) * LANES] = y


def rope_tables(pos):
    half = B_ROT // 2
    inv = ROPE_THETA ** (-jnp.arange(half, dtype=F32) * 2.0 / B_ROT)
    ang = pos.astype(F32)[:, None] * inv[None, :]
    cos, sin = jnp.cos(ang), jnp.sin(ang)
    n = pos.shape[0]
    one = jnp.ones((n, B_QK - B_ROT), F32)
    zero = jnp.zeros((n, B_QK - B_ROT), F32)
    zh = jnp.zeros((n, half), F32)
    comp = lambda a, b_, rest: jnp.concatenate([a, b_, rest], axis=1)
    cos_p = comp(cos, cos, one)
    sin_a = comp(zh, sin, zero)
    sin_b = comp(-sin, zh, zero)
    dup = lambda x: jnp.concatenate([x, x], axis=1)
    return dup(cos_p), dup(sin_a), dup(sin_b)


def rope_qk(ub, tables, tm=256):
    bsz, seq_len, _ = ub.shape
    tm = min(tm, seq_len)
    nslab = 2 * B_WIDTH // LANES
    tab = pl.BlockSpec((tm, LANES), lambda b, j: (j, 0))
    return pl.pallas_call(
        functools.partial(_rope_body, nslab=nslab),
        grid=(bsz, seq_len // tm),
        in_specs=[pl.BlockSpec((1, tm, 2 * B_WIDTH), lambda b, j: (b, j, 0)), tab, tab, tab],
        out_specs=pl.BlockSpec((1, tm, 2 * B_WIDTH), lambda b, j: (b, j, 0)),
        out_shape=jax.ShapeDtypeStruct((bsz, seq_len, 2 * B_WIDTH), F32),
        compiler_params=_cparams(("parallel", "parallel")),
        name="rope_qk",
    )(ub, *tables)


def _lam(lqk_ref, lam_init):
    d1 = jnp.sum(lqk_ref[0:1, :] * lqk_ref[1:2, :], axis=-1, keepdims=True)
    d2 = jnp.sum(lqk_ref[2:3, :] * lqk_ref[3:4, :], axis=-1, keepdims=True)
    return jnp.exp(d1) - jnp.exp(d2) + lam_init


def _diff_finish(acc1, l1, acc2, l2, lam, subln, lam_init):
    a = acc1 / l1 - lam * (acc2 / l2)
    y = a * lax.rsqrt(jnp.mean(a * a, axis=-1, keepdims=True) + EPS) * subln
    return y * (1.0 - lam_init)


_NT = (((1,), (1,)), ((), ()))


def _attn_prompt_body(q_ref, k_ref, v_ref, lqk_ref, sub_ref, o_ref, qs_ref, m_ref, l_ref, acc_ref,
                      *, tq, tk, hp, lam_init):
    qi = pl.program_id(2)
    ki = pl.program_id(3)
    nchain = 2 * hp
    head = lambda h: slice(h * B_VDIM, (h + 1) * B_VDIM)

    @pl.when(ki == 0)
    def _():
        q = q_ref[0] * (B_QK ** -0.5)
        lane = lax.broadcasted_iota(jnp.int32, (tq, B_VDIM), 1)
        for h in range(hp):
            qh = q[:, head(h)]
            qs_ref[2 * h] = jnp.where(lane < B_QK, qh, 0.0).astype(BF16)
            qs_ref[2 * h + 1] = jnp.where(lane >= B_QK, qh, 0.0).astype(BF16)
        m_ref[...] = jnp.full(m_ref.shape, NEG_INF, F32)
        l_ref[...] = jnp.zeros(l_ref.shape, F32)
        acc_ref[...] = jnp.zeros(acc_ref.shape, F32)

    def block(masked):
        ks = [k_ref[0, :, head(h)].astype(BF16) for h in range(hp)]
        vs = [v_ref[0, :, head(h)].astype(BF16) for h in range(hp)]
        ss = [lax.dot_general(qs_ref[c], ks[c // 2], _NT, preferred_element_type=F32) for c in range(nchain)]
        if masked:
            row = qi * tq + lax.broadcasted_iota(jnp.int32, (tq, tk), 0)
            col = ki * tk + lax.broadcasted_iota(jnp.int32, (tq, tk), 1)
            keep = col <= row
            ss = [jnp.where(keep, s, NEG_INF) for s in ss]
        alphas, ps = [], []
        for c, s in enumerate(ss):
            m_prev = m_ref[c]
            m_new = jnp.maximum(m_prev, jnp.max(s, axis=-1, keepdims=True))
            alpha = jnp.exp(m_prev - m_new)
            p = jnp.exp(s - m_new)
            l_ref[c] = alpha * l_ref[c] + jnp.sum(p, axis=-1, keepdims=True)
            m_ref[c] = m_new
            alphas.append(alpha)
            ps.append(p.astype(BF16))
        for c in range(nchain):
            acc_ref[c] = alphas[c] * acc_ref[c] + jnp.dot(ps[c], vs[c // 2], preferred_element_type=F32)

    visible = ki * tk < (qi + 1) * tq
    full = ki * tk + tk - 1 <= qi * tq

    @pl.when(full)
    def _():
        block(False)

    @pl.when(visible & jnp.logical_not(full))
    def _():
        block(True)

    @pl.when(ki == pl.num_programs(3) - 1)
    def _():
        lam = _lam(lqk_ref, lam_init)
        for h in range(hp):
            y = _diff_finish(acc_ref[2 * h], l_ref[2 * h], acc_ref[2 * h + 1], l_ref[2 * h + 1], lam,
                             sub_ref[...], lam_init)
            o_ref[0, :, head(h)] = y.astype(o_ref.dtype)


def attn_prompt(qk, ub, lqk, subln, lam_init, tq=512, tk=512, hp=2):
    bsz, seq_len, _ = qk.shape
    tq = min(tq, seq_len)
    tk = min(tk, seq_len)
    assert seq_len % tq == 0 and seq_len % tk == 0 and B_HEADS % hp == 0
    nq, nk = seq_len // tq, seq_len // tk
    ngrp = B_HEADS // hp
    kv_blk = lambda qi, ki: jnp.minimum(ki, ((qi + 1) * tq - 1) // tk)
    wide = hp * B_VDIM
    return pl.pallas_call(
        functools.partial(_attn_prompt_body, tq=tq, tk=tk, hp=hp, lam_init=lam_init),
        grid=(bsz, ngrp, nq, nk),
        in_specs=[
            pl.BlockSpec((1, tq, wide), lambda b, g, qi, ki: (b, qi, g)),
            pl.BlockSpec((1, tk, wide), lambda b, g, qi, ki: (b, kv_blk(qi, ki), ngrp + g)),
            pl.BlockSpec((1, tk, wide), lambda b, g, qi, ki: (b, kv_blk(qi, ki), 2 * ngrp + g)),
            pl.BlockSpec((4, B_QK), lambda b, g, qi, ki: (0, 0)),
            pl.BlockSpec((1, B_VDIM), lambda b, g, qi, ki: (0, 0)),
        ],
        out_specs=pl.BlockSpec((1, tq, wide), lambda b, g, qi, ki: (b, qi, g)),
        out_shape=jax.ShapeDtypeStruct((bsz, seq_len, B_WIDTH), BF16),
        scratch_shapes=[pltpu.VMEM((2 * hp, tq, B_VDIM), BF16), pltpu.VMEM((2 * hp, tq, 1), F32),
                        pltpu.VMEM((2 * hp, tq, 1), F32), pltpu.VMEM((2 * hp, tq, B_VDIM), F32)],
        compiler_params=_cparams(("parallel", "parallel", "parallel", "arbitrary")),
        name="attn_prompt",
    )(qk, qk, ub, lqk, subln.reshape(1, B_VDIM))


def _attn_decode_body(pt_ref, q_ref, kn_ref, vn_ref, ck_ref, cv_ref, lqk_ref, sub_ref, o_ref,
                      qa_ref, m_ref, l_ref, acc_ref, bias_ref, *, nq, n_pages, lam_init):
    del pt_ref
    p = pl.program_id(1)
    rows = 2 * nq
    nrow = B_HEADS * rows
    bf_round = lambda x: x.astype(BF16).astype(F32)

    @pl.when(p == 0)
    def _():
        q = q_ref[0] * (B_QK ** -0.5)
        lane = lax.broadcasted_iota(jnp.int32, (nq, B_VDIM), 1)
        for h in range(B_HEADS):
            qh = q[:, h * B_VDIM:(h + 1) * B_VDIM]
            qa_ref[h * rows:(h + 1) * rows, :] = jnp.concatenate(
                [jnp.where(lane < B_QK, qh, 0.0), jnp.where(lane >= B_QK, qh, 0.0)], axis=0)
        m_ref[...] = jnp.full(m_ref.shape, NEG_INF, F32)
        l_ref[...] = jnp.zeros(l_ref.shape, F32)
        acc_ref[...] = jnp.zeros(acc_ref.shape, F32)
        own = (lax.broadcasted_iota(jnp.int32, bias_ref.shape, 0) // rows
               == lax.broadcasted_iota(jnp.int32, bias_ref.shape, 1) % B_HEADS)
        bias_ref[...] = jnp.where(own, 0.0, NEG_INF)

    @pl.when(p < n_pages)
    def _():
        page = ck_ref.shape[2]
        ncol = page * B_HEADS
        k = ck_ref[0, 0].reshape(ncol, B_VDIM).astype(BF16)
        v = cv_ref[0, 0].reshape(ncol, B_VDIM).astype(BF16)
        s = lax.dot_general(qa_ref[...].astype(BF16), k, _NT, preferred_element_type=F32) + bias_ref[...]
        m_prev = m_ref[...]
        m_new = jnp.maximum(m_prev, jnp.max(s, axis=-1, keepdims=True))
        alpha = jnp.exp(m_prev - m_new)
        pr = jnp.exp(s - m_new)
        l_ref[...] = alpha * l_ref[...] + jnp.sum(pr, axis=-1, keepdims=True)
        acc_ref[...] = alpha * acc_ref[...] + jnp.dot(pr.astype(BF16), v, preferred_element_type=F32)
        m_ref[...] = m_new

    @pl.when(p == n_pages)
    def _():
        lam = _lam(lqk_ref, lam_init)
        qrow = lax.broadcasted_iota(jnp.int32, (nrow, 1), 0) % nq
        qa = bf_round(qa_ref[...])
        kn = bf_round(kn_ref[0])
        vn = bf_round(vn_ref[0])
        per_head = lambda x, j: jnp.concatenate(
            [jnp.broadcast_to(x[j:j + 1, h * B_VDIM:(h + 1) * B_VDIM], (rows, B_VDIM)) for h in range(B_HEADS)], axis=0)
        ss = [jnp.where(qrow >= j, jnp.sum(qa * per_head(kn, j), axis=-1, keepdims=True), NEG_INF)
              for j in range(nq)]
        m_prev = m_ref[...]
        m_new = m_prev
        for sj in ss:
            m_new = jnp.maximum(m_new, sj)
        alpha = jnp.exp(m_prev - m_new)
        l = alpha * l_ref[...]
        acc = alpha * acc_ref[...]
        for j, sj in enumerate(ss):
            pj = jnp.exp(sj - m_new)
            l = l + pj
            acc = acc + bf_round(pj) * per_head(vn, j)
        n = acc / l
        a = n - lam * pltpu.roll(n, nrow - nq, 0)
        y = a * lax.rsqrt(jnp.mean(a * a, axis=-1, keepdims=True) + EPS) * sub_ref[...] * (1.0 - lam_init)
        for h in range(B_HEADS):
            o_ref[0, :, h * B_VDIM:(h + 1) * B_VDIM] = y[h * rows:h * rows + nq].astype(o_ref.dtype)


def attn_decode(qk, ub, cache_k, cache_v, page_table, layer, lqk, subln, lam_init):
    bsz, nq, _ = qk.shape
    n_pages = page_table.shape[1]
    page = cache_k.shape[2]
    nrow = B_HEADS * 2 * nq
    cache_spec = pl.BlockSpec(
        (1, 1, page, B_HEADS, B_VDIM),
        lambda b, p, pt: (layer, pt[b * n_pages + jnp.minimum(p, n_pages - 1)], 0, 0, 0))
    grid_spec = pltpu.PrefetchScalarGridSpec(
        num_scalar_prefetch=1,
        grid=(bsz, n_pages + 1),
        in_specs=[
            pl.BlockSpec((1, nq, B_WIDTH), lambda b, p, pt: (b, 0, 0)),
            pl.BlockSpec((1, nq, B_WIDTH), lambda b, p, pt: (b, 0, 1)),
            pl.BlockSpec((1, nq, B_WIDTH), lambda b, p, pt: (b, 0, 2)),
            cache_spec, cache_spec,
            pl.BlockSpec((4, B_QK), lambda b, p, pt: (0, 0)),
            pl.BlockSpec((1, B_VDIM), lambda b, p, pt: (0, 0)),
        ],
        out_specs=pl.BlockSpec((1, nq, B_WIDTH), lambda b, p, pt: (b, 0, 0)),
        scratch_shapes=[
            pltpu.VMEM((nrow, B_VDIM), F32),
            pltpu.VMEM((nrow, 1), F32),
            pltpu.VMEM((nrow, 1), F32),
            pltpu.VMEM((nrow, B_VDIM), F32),
            pltpu.VMEM((nrow, page * B_HEADS), F32),
        ],
    )
    return pl.pallas_call(
        functools.partial(_attn_decode_body, nq=nq, n_pages=n_pages, lam_init=lam_init),
        grid_spec=grid_spec,
        out_shape=jax.ShapeDtypeStruct((bsz, nq, B_WIDTH), F32),
        compiler_params=_cparams(("parallel", "arbitrary")),
        name="attn_decode",
    )(page_table.reshape(-1), qk, qk, ub, cache_k, cache_v, lqk, subln.reshape(1, B_VDIM)).astype(BF16)


_TN = (((0,), (0,)), ((), ()))


def _hgrn_body(q_ref, f_ref, i_ref, g_ref, lb_ref, cn_ref, s0_ref, o_ref, sfin_ref, s_ref, *, tc, n_valid):
    c = pl.program_id(2)

    @pl.when(c == 0)
    def _():
        s_ref[...] = s0_ref[0, 0]

    nrow = -(-tc // C_CHUNK) * C_CHUNK
    pad = lambda x: x if nrow == tc else jnp.concatenate([x, jnp.zeros((nrow - tc, x.shape[1]), F32)], axis=0)
    q = _silu(pad(q_ref[0]))
    lb = lb_ref[...]
    fs = lb + (1.0 - lb) * _sigmoid(pad(f_ref[0]))
    kin = 1.0 - fs
    gl = jnp.log(fs)
    iv = pad(i_ref[0])
    row = lax.broadcasted_iota(jnp.int32, (nrow, C_DK), 0)
    if n_valid < nrow:
        kin = jnp.where(row < n_valid, kin, 0.0)
        gl = jnp.where(row < n_valid, gl, 0.0)
    b = gl
    sh = 1
    while sh < C_CHUNK:
        b = b + jnp.where(row % C_CHUNK >= sh, pltpu.roll(b, sh, 0), 0.0)
        sh *= 2
    trow = lax.broadcasted_iota(jnp.int32, (C_CHUNK, C_DK), 0)
    eye = (lax.broadcasted_iota(jnp.int32, (C_DK, C_DK), 0)
           == lax.broadcasted_iota(jnp.int32, (C_DK, C_DK), 1)).astype(F32)
    outs = []
    for ch in range(nrow // C_CHUNK):
        rs = slice(ch * C_CHUNK, (ch + 1) * C_CHUNK)
        bc, qc, kc, ic = b[rs], q[rs], kin[rs], iv[rs]
        s = s_ref[...]
        o = jnp.dot((qc * jnp.exp(bc)).astype(BF16), s.astype(BF16), preferred_element_type=F32)
        for sidx in range(C_CHUNK):
            dec = jnp.exp(jnp.where(trow >= sidx, bc - bc[sidx:sidx + 1, :], NEG_INF))
            att = jnp.sum(qc * kc[sidx:sidx + 1, :] * dec, axis=-1, keepdims=True)
            o = o + att * ic[sidx:sidx + 1, :]
        bl = bc[C_CHUNK - 1:C_CHUNK, :]
        kd = kc * jnp.exp(bl - bc)
        inc = lax.dot_general(kd.astype(BF16), ic.astype(BF16), _TN, preferred_element_type=F32)
        ecol = jnp.sum(eye * jnp.exp(bl), axis=1, keepdims=True)
        s_ref[...] = s * ecol + inc
        outs.append(o)
    o = jnp.concatenate(outs, axis=0)[0:tc]
    y = o * lax.rsqrt(jnp.mean(o * o, axis=-1, keepdims=True) + EPS) * cn_ref[...]
    o_ref[0] = (y * _silu(g_ref[0])).astype(o_ref.dtype)

    @pl.when(c == pl.num_programs(2) - 1)
    def _():
        sfin_ref[0, 0] = s_ref[...]


def hgrn(uc, s0, lb, c_norm, tc=128):
    bsz, seq_len, _ = uc.shape
    tc = min(tc, seq_len)
    assert seq_len % tc == 0
    tok = lambda off: pl.BlockSpec((1, tc, LANES), lambda b, h, c: (b, c, off + h))
    st = pl.BlockSpec((1, 1, C_DK, C_DV), lambda b, h, c: (b, h, 0, 0))
    return pl.pallas_call(
        functools.partial(_hgrn_body, tc=tc, n_valid=tc),
        grid=(bsz, C_HEADS, seq_len // tc),
        in_specs=[tok(0), tok(C_HEADS), tok(2 * C_HEADS), tok(3 * C_HEADS),
                  pl.BlockSpec((1, LANES), lambda b, h, c: (0, h)),
                  pl.BlockSpec((1, C_DV), lambda b, h, c: (0, 0)),
                  st],
        out_specs=[pl.BlockSpec((1, tc, LANES), lambda b, h, c: (b, c, h)), st],
        out_shape=[jax.ShapeDtypeStruct((bsz, seq_len, C_WIDTH), BF16),
                   jax.ShapeDtypeStruct((bsz, C_HEADS, C_DK, C_DV), F32)],
        scratch_shapes=[pltpu.VMEM((C_DK, C_DV), F32)],
        compiler_params=_cparams(("parallel", "parallel", "arbitrary")),
        name="hgrn",
    )(uc, uc, uc, uc, lb.reshape(1, C_FK), c_norm.reshape(1, C_DV), s0)


def _pair_state(s):
    bsz = s.shape[0]
    return s.reshape(bsz, A_PAIRS, 2, A_HEAD, A_HEAD).transpose(0, 1, 3, 2, 4).reshape(bsz, A_PAIRS, A_HEAD, LANES)


def _unpair_state(s):
    bsz = s.shape[0]
    return s.reshape(bsz, A_PAIRS, A_HEAD, 2, A_HEAD).transpose(0, 1, 3, 2, 4).reshape(bsz, A_HEADS, A_HEAD, A_HEAD)


def _layer_params(l, P):
    w_in = P["w_in"][l]
    b0 = A_COLS
    c0 = A_COLS + 3 * B_WIDTH
    padc = A_COLS_PAD - A_COLS
    w2 = jnp.zeros((LANES, 2 * A_WIDTH), F32)
    w2 = w2.at[0:A_DECAY_R, 0:A_WIDTH].set(P["a_w_up"][l])
    w2 = w2.at[A_DECAY_R:A_DECAY_R + A_ICLR_R, A_WIDTH:].set(P["a_a_up"][l])
    return dict(
        w_a=jnp.pad(w_in[:, :b0], ((0, 0), (0, padc))),
        w_b=w_in[:, b0:c0],
        w_c=w_in[:, c0:],
        mu=jnp.pad(P["a_mu"][l], (0, padc)),
        w2=w2,
        gup=jnp.pad(P["a_g_up"][l], ((0, 2 * LANES - A_GATE_R), (0, 0))),
        lqk=jnp.stack([P["b_lq1"][l], P["b_lk1"][l], P["b_lq2"][l], P["b_lk2"][l]]),
        r_k=P["a_r_k"][l].reshape(A_WIDTH),
    )


def _run_trunk(x, pos, caches, wkv0, shift0, hgrn0, conv0, lbs, P, LP):
    bsz, seq_len, d = x.shape
    t = bsz * seq_len
    h = x.reshape(t, d)
    tables = rope_tables(pos)
    padc = A_COLS_PAD - A_COLS
    ks, vs, wkvs, shifts, hgrns, convs = [], [], [], [], [], []
    for l in range(DEPTH):
        lp = LP[l]
        hn = rmsnorm_rows(h, P["norm_mix"][l], BF16)
        ua = matmul(hn, lp["w_a"]).reshape(bsz, seq_len, A_COLS_PAD)
        ub = matmul(hn, lp["w_b"]).reshape(bsz, seq_len, 3 * B_WIDTH)
        uc = matmul(hn, lp["w_c"]).reshape(bsz, seq_len, 4 * C_WIDTH)
        r, dcy, kp, v, kk, bb, g, bonus = rwkv_pre(
            ua, jnp.pad(shift0[l], ((0, 0), (0, padc))), lp["mu"], lp["w2"], lp["gup"],
            P["a_w0"][l], P["a_a0"][l], P["a_k_k"][l], P["a_k_a"][l], lp["r_k"])
        o_a, s_a = rwkv_scan(r, dcy, kp, v, kk, bb, _pair_state(wkv0[l]))
        flat = lambda z: z.reshape(t, A_WIDTH)
        oa = rwkv_post(flat(o_a), flat(bonus), flat(g), P["a_ln_w"][l], P["a_ln_b"][l])
        lam_init = 0.8 - 0.6 * math.exp(-0.3 * l)
        qk = rope_qk(ub, tables)
        if caches is None:
            ob = attn_prompt(qk, ub, lp["lqk"], P["b_subln"][l], lam_init)
        else:
            ob = attn_decode(qk, ub, caches[0], caches[1], caches[2], l, lp["lqk"], P["b_subln"][l], lam_init)
        oc, s_c = hgrn(uc, hgrn0[l], lbs[l], P["c_norm"][l])
        mix = jnp.concatenate([oa, ob.reshape(t, B_WIDTH), oc.reshape(t, C_WIDTH)], axis=-1)
        h = matmul(mix, P["w_out"][l], res=h)
        hf = rmsnorm_rows(h, P["norm_ffn"][l], BF16)
        mid, buf = ffn_up(hf, P["f_gate"][l], P["f_up"][l], P["f_conv"][l], P["f_conv_b"][l], conv0[l], seq_len)
        h = matmul(mid, P["f_down"][l], res=h, tn=512, tk=1024)
        ks.append(qk[..., B_WIDTH:].reshape(bsz, seq_len, B_HEADS, B_VDIM))
        vs.append(ub[..., 2 * B_WIDTH:].reshape(bsz, seq_len, B_HEADS, B_VDIM))
        wkvs.append(_unpair_state(s_a))
        shifts.append(ua[:, -1, :A_COLS])
        hgrns.append(s_c)
        convs.append(buf)
    y = rmsnorm_rows(h, P["norm_final"], F32).reshape(bsz, seq_len, d)
    st = jnp.stack
    return y, st(ks), st(vs), st(wkvs), st(shifts), st(hgrns), st(convs)


def kernel(x_prompt, x_sample, cache_k, cache_v, page_table, state_wkv, state_shift, state_hgrn, state_conv,
           norm_mix, w_in, a_mu, a_w0, a_w_up, a_a0, a_a_up, a_g_up, a_k_k, a_k_a, a_r_k, a_ln_w, a_ln_b,
           b_lq1, b_lk1, b_lq2, b_lk2, b_subln, c_lb_logits, c_norm, w_out, norm_ffn, f_gate, f_up,
           f_conv, f_conv_b, f_down, norm_final):
    P = dict(norm_mix=norm_mix, w_in=w_in, a_mu=a_mu, a_w0=a_w0, a_w_up=a_w_up, a_a0=a_a0, a_a_up=a_a_up,
             a_g_up=a_g_up, a_k_k=a_k_k, a_k_a=a_k_a, a_r_k=a_r_k, a_ln_w=a_ln_w, a_ln_b=a_ln_b,
             b_lq1=b_lq1, b_lk1=b_lk1, b_lq2=b_lq2, b_lk2=b_lk2, b_subln=b_subln, c_norm=c_norm,
             w_out=w_out, norm_ffn=norm_ffn, f_gate=f_gate, f_up=f_up, f_conv=f_conv, f_conv_b=f_conv_b,
             f_down=f_down, norm_final=norm_final)
    LP = [_layer_params(l, P) for l in range(DEPTH)]
    lbs = lower_bounds(c_lb_logits)
    bp, lp_ = x_prompt.shape[:2]
    past_len = page_table.shape[1] * cache_k.shape[2]
    pos_p = jnp.arange(lp_)
    pos_s = past_len + jnp.arange(x_sample.shape[1])
    zeros = lambda *s: jnp.zeros(s, F32)
    out_p = _run_trunk(
        x_prompt, pos_p, None,
        zeros(DEPTH, bp, A_HEADS, A_HEAD, A_HEAD), zeros(DEPTH, bp, A_COLS),
        zeros(DEPTH, bp, C_HEADS, C_DK, C_DV), zeros(DEPTH, bp, CONV_W - 1, D_FF), lbs, P, LP)
    caches = (cache_k, cache_v, page_table)
    out_s = _run_trunk(x_sample, pos_s, caches, state_wkv, state_shift, state_hgrn, state_conv, lbs, P, LP)
    y_p, k_p, v_p, wkv_p, sh_p, hg_p, cv_p = out_p
    y_s, k_s, v_s, wkv_s, sh_s, hg_s, cv_s = out_s
    return (y_p, y_s, k_p, v_p, k_s, v_s, wkv_p, wkv_s, sh_p, sh_s, hg_p, hg_s, cv_p, cv_s)
```

```python
import functools
import math

import jax
import jax.numpy as jnp
from jax import lax
from jax.experimental import pallas as pl
from jax.experimental.pallas import tpu as pltpu

F32 = jnp.float32
BF16 = jnp.bfloat16

V7X_VMEM_BYTES = 64 * 1024 * 1024
VMEM_LIMIT = V7X_VMEM_BYTES - 8 * 1024 * 1024
LANES = 128
SUBLANES = 8

D_MODEL = 4096
DEPTH = 4
A_WIDTH = D_MODEL // 4
A_HEAD = 64
A_HEADS = A_WIDTH // A_HEAD
A_PAIRS = A_HEADS // 2
A_DECAY_R = 64
A_ICLR_R = 64
A_GATE_R = 160
A_COLS = 3 * A_WIDTH + A_DECAY_R + A_ICLR_R + A_GATE_R
A_COLS_PAD = 3 * A_WIDTH + 3 * LANES
A_GN_EPS = 64e-5
B_WIDTH = D_MODEL // 2
B_HEADS = 16
B_VDIM = B_WIDTH // B_HEADS
B_QK = B_VDIM // 2
B_ROT = B_QK // 4
ROPE_THETA = 500000.0
C_WIDTH = D_MODEL - A_WIDTH - B_WIDTH
C_HEADS = 8
C_DK = 128
C_DV = C_WIDTH // C_HEADS
C_FK = C_HEADS * C_DK
C_CHUNK = 16
D_FF = 11008
CONV_W = 3
EPS = 1e-5
NEG_INF = float("-inf")


def _cparams(sem):
    return pltpu.CompilerParams(dimension_semantics=sem, vmem_limit_bytes=VMEM_LIMIT)


def _sigmoid(x):
    return 1.0 / (1.0 + jnp.exp(-x))


def _silu(x):
    return x * _sigmoid(x)


def _block_ones(dtype):
    r = lax.broadcasted_iota(jnp.int32, (LANES, LANES), 0) // A_HEAD
    c = lax.broadcasted_iota(jnp.int32, (LANES, LANES), 1) // A_HEAD
    return (r == c).astype(dtype)


def _rmsnorm_body(x_ref, w_ref, o_ref):
    x = x_ref[...]
    y = x * lax.rsqrt(jnp.mean(x * x, axis=-1, keepdims=True) + EPS)
    o_ref[...] = (y * w_ref[...]).astype(o_ref.dtype)


def rmsnorm_rows(x, w, out_dtype, tr=256):
    t, d = x.shape
    tr = min(tr, t)
    return pl.pallas_call(
        _rmsnorm_body,
        grid=(pl.cdiv(t, tr),),
        in_specs=[pl.BlockSpec((tr, d), lambda i: (i, 0)), pl.BlockSpec((1, d), lambda i: (0, 0))],
        out_specs=pl.BlockSpec((tr, d), lambda i: (i, 0)),
        out_shape=jax.ShapeDtypeStruct((t, d), out_dtype),
        compiler_params=_cparams(("parallel",)),
        name="rmsnorm_rows",
    )(x, w.reshape(1, d))


def _mm_body(*refs, nk, tk, k_total, has_res):
    if has_res:
        a_ref, w_ref, r_ref, o_ref = refs
    else:
        a_ref, w_ref, o_ref = refs
        r_ref = None
    a = a_ref[...]
    w = w_ref[...].astype(BF16)
    if nk == 1:
        acc = jnp.dot(a, w, preferred_element_type=F32)
        if has_res:
            acc = acc + r_ref[...]
        o_ref[...] = acc.astype(o_ref.dtype)
        return
    k = pl.program_id(2)
    if k_total % tk:
        valid = k_total - k * tk
        a = jnp.where(lax.broadcasted_iota(jnp.int32, a.shape, 1) < valid, a, jnp.zeros_like(a))
        w = jnp.where(lax.broadcasted_iota(jnp.int32, w.shape, 0) < valid, w, jnp.zeros_like(w))
    part = jnp.dot(a, w, preferred_element_type=F32)

    @pl.when(k == 0)
    def _():
        o_ref[...] = (part + r_ref[...]) if has_res else part

    @pl.when(k > 0)
    def _():
        o_ref[...] += part


def matmul(a, w, layer=None, res=None, out_dtype=F32, tm=2048, tn=256, tk=None, pad_n=False):
    m, kdim = a.shape
    n = w.shape[-1]
    tm = min(tm, m)
    tk = kdim if tk is None else tk
    nk = pl.cdiv(kdim, tk)
    if nk > 1:
        assert out_dtype == F32
    grid = (pl.cdiv(m, tm), pl.cdiv(n, tn), nk)
    n_out = grid[1] * tn if pad_n else n
    once = dict(pipeline_mode=pl.Buffered(1))
    a_kw = once if (nk == 1 and grid[1] > 1) else {}
    if layer is None:
        w_spec = pl.BlockSpec((tk, tn), lambda i, j, k: (k, j))
    else:
        w_spec = pl.BlockSpec((None, tk, tn), lambda i, j, k: (layer, k, j))
    in_specs = [pl.BlockSpec((tm, tk), lambda i, j, k: (i, k), **a_kw), w_spec]
    args = [a, w]
    if res is not None:
        in_specs.append(pl.BlockSpec((tm, tn), lambda i, j, k: (i, j), **(once if nk > 1 else {})))
        args.append(res)
    return pl.pallas_call(
        functools.partial(_mm_body, nk=nk, tk=tk, k_total=kdim, has_res=res is not None),
        grid=grid,
        in_specs=in_specs,
        out_specs=pl.BlockSpec((tm, tn), lambda i, j, k: (i, j)),
        out_shape=jax.ShapeDtypeStruct((m, n_out), out_dtype),
        compiler_params=_cparams(("parallel", "parallel", "arbitrary")),
        name="matmul",
    )(*args)


def _ffn_up_body(a_ref, wg_ref, wu_ref, cw_ref, cb_ref, buf_ref, o_ref, nbuf_ref, ext_ref, *, nb, seq_len):
    tm = nb * seq_len
    a = a_ref[...]
    gate = jnp.dot(a, wg_ref[...].astype(BF16), preferred_element_type=F32)
    up = jnp.dot(a, wu_ref[...].astype(BF16), preferred_element_type=F32)
    ext_ref[0:SUBLANES, :] = jnp.zeros((SUBLANES, gate.shape[1]), F32)
    ext_ref[SUBLANES:SUBLANES + tm, :] = gate
    g1 = ext_ref[SUBLANES - 1:SUBLANES - 1 + tm, :]
    g2 = ext_ref[SUBLANES - 2:SUBLANES - 2 + tm, :]
    row = lax.broadcasted_iota(jnp.int32, gate.shape, 0)
    for s in range(nb):
        b0 = buf_ref[s, 0:1, :]
        b1 = buf_ref[s, 1:2, :]
        g1 = jnp.where(row == s * seq_len, b1, g1)
        g2 = jnp.where(row == s * seq_len, b0, g2)
        g2 = jnp.where(row == s * seq_len + 1, b1, g2)
        last = SUBLANES + (s + 1) * seq_len
        nbuf_ref[s, :, :] = ext_ref[last - 2:last, :]
    conv = cb_ref[...] + g2 * cw_ref[0:1, :] + g1 * cw_ref[1:2, :] + gate * cw_ref[2:3, :]
    o_ref[...] = (_silu(conv) * up).astype(o_ref.dtype)


def ffn_up(hn, w_gate, w_up, w_conv, b_conv, buf, seq_len, layer=None, tn=256):
    t, d = hn.shape
    n = w_gate.shape[-1]
    nseq = buf.shape[0]
    nb = 1 if seq_len >= 256 else nseq
    tm = nb * seq_len
    assert seq_len >= 2 and t == nseq * seq_len and n % tn == 0
    grid = (t // tm, n // tn)
    if layer is None:
        wspec = lambda rows: pl.BlockSpec((rows, tn), lambda i, j: (0, j))
        b_conv = b_conv.reshape(1, n)
    else:
        wspec = lambda rows: pl.BlockSpec((None, rows, tn), lambda i, j: (layer, 0, j))
        b_conv = b_conv.reshape(-1, 1, n)
    return pl.pallas_call(
        functools.partial(_ffn_up_body, nb=nb, seq_len=seq_len),
        grid=grid,
        in_specs=[
            pl.BlockSpec((tm, d), lambda i, j: (i, 0), pipeline_mode=pl.Buffered(1)),
            wspec(d), wspec(d), wspec(CONV_W), wspec(1),
            pl.BlockSpec((nb, CONV_W - 1, tn), lambda i, j: (i, 0, j)),
        ],
        out_specs=[
            pl.BlockSpec((tm, tn), lambda i, j: (i, j)),
            pl.BlockSpec((nb, CONV_W - 1, tn), lambda i, j: (i, 0, j)),
        ],
        out_shape=[
            jax.ShapeDtypeStruct((t, n), BF16),
            jax.ShapeDtypeStruct((nseq, CONV_W - 1, n), F32),
        ],
        scratch_shapes=[pltpu.VMEM((SUBLANES + tm, tn), F32)],
        compiler_params=_cparams(("parallel", "parallel")),
        name="ffn_up",
    )(hn, w_gate, w_up, w_conv, b_conv, buf)


def _lower_bounds_body(x_ref, o_ref):
    x = x_ref[...]
    e = jnp.exp(x - jnp.max(x, axis=0, keepdims=True))
    p = e / jnp.sum(e, axis=0, keepdims=True)
    run = jnp.zeros_like(p[0:1, :])
    for l in range(x.shape[0]):
        run = run + p[l:l + 1, :]
        o_ref[l:l + 1, :] = run - p[0:1, :]


def lower_bounds(logits):
    return pl.pallas_call(
        _lower_bounds_body,
        out_shape=jax.ShapeDtypeStruct(logits.shape, F32),
        name="lower_bounds",
    )(logits)


def _seg_sum(x, ones):
    parts = []
    for s in range(x.shape[1] // LANES):
        parts.append(jnp.dot(x[:, s * LANES:(s + 1) * LANES], ones,
                             preferred_element_type=F32, precision=lax.Precision.HIGHEST))
    return jnp.concatenate(parts, axis=1)


def _rwkv_pre_body(u_ref, prev_ref, mu_ref, w2_ref, gup_ref, w0_ref, a0_ref, kk_ref, ka_ref, rk_ref,
                   r_o, d_o, k_o, v_o, kk_o, b_o, g_o, bonus_o, ext_ref, *, tm):
    j = pl.program_id(1)
    u = u_ref[0]

    @pl.when(j == 0)
    def _():
        ext_ref[SUBLANES - 1:SUBLANES, :] = prev_ref[0]

    ext_ref[SUBLANES:SUBLANES + tm, :] = u
    shifted = ext_ref[SUBLANES - 1:SUBLANES - 1 + tm, :]
    ext_ref[SUBLANES - 1:SUBLANES, :] = u[tm - 1:tm, :]
    xm = u + (shifted - u) * mu_ref[...]
    w = A_WIDTH
    r = xm[:, 0:w]
    k = xm[:, w:2 * w]
    v = xm[:, 2 * w:3 * w]
    lr = xm[:, 3 * w:3 * w + LANES]
    gd = xm[:, 3 * w + LANES:3 * w + 3 * LANES]
    lane = lax.broadcasted_iota(jnp.int32, lr.shape, 1)
    lr = jnp.where(lane < A_DECAY_R, jnp.tanh(lr), lr)
    wa = jnp.dot(lr.astype(BF16), w2_ref[...].astype(BF16), preferred_element_type=F32)
    wraw = -(w0_ref[...] + wa[:, 0:w])
    softplus = jnp.maximum(wraw, 0.0) + jnp.log(1.0 + jnp.exp(-jnp.abs(wraw)))
    decay = jnp.exp(-jnp.exp(-softplus - 0.5))
    a = _sigmoid(a0_ref[...] + wa[:, w:2 * w])
    g = jnp.dot(_sigmoid(gd).astype(BF16), gup_ref[...].astype(BF16), preferred_element_type=F32)
    ones = _block_ones(F32)
    kk = k * kk_ref[...]
    norm = jnp.sqrt(_seg_sum(kk * kk, ones))
    kk = kk / jnp.maximum(norm, 1e-12)
    kp = k * (1.0 + (a - 1.0) * ka_ref[...])
    bonus = _seg_sum(r * kp * rk_ref[...], ones) * v
    r_o[0] = r
    d_o[0] = decay
    k_o[0] = kp
    v_o[0] = v
    kk_o[0] = kk
    b_o[0] = kk * a
    g_o[0] = g
    bonus_o[0] = bonus


def rwkv_pre(ua, prev, mu, w2, gup, w0, a0, k_k, k_a, r_k, tm=256):
    bsz, seq_len, width = ua.shape
    cols = A_COLS_PAD
    tm = min(tm, seq_len)
    assert seq_len % tm == 0 and width % cols == 0
    w = A_WIDTH
    row = lambda x: x.reshape(1, -1)
    vec = lambda n: pl.BlockSpec((1, n), lambda b, j: (0, 0))
    out_spec = pl.BlockSpec((1, tm, w), lambda b, j: (b, j, 0))
    out_sds = jax.ShapeDtypeStruct((bsz, seq_len, w), F32)
    return pl.pallas_call(
        functools.partial(_rwkv_pre_body, tm=tm),
        grid=(bsz, seq_len // tm),
        in_specs=[
            pl.BlockSpec((1, tm, cols), lambda b, j: (b, j, 0)),
            pl.BlockSpec((1, 1, cols), lambda b, j: (b, 0, 0)),
            vec(cols),
            pl.BlockSpec((LANES, 2 * w), lambda b, j: (0, 0)),
            pl.BlockSpec((2 * LANES, w), lambda b, j: (0, 0)),
            vec(w), vec(w), vec(w), vec(w), vec(w),
        ],
        out_specs=[out_spec] * 8,
        out_shape=[out_sds] * 8,
        scratch_shapes=[pltpu.VMEM((SUBLANES + tm, cols), F32)],
        compiler_params=_cparams(("parallel", "arbitrary")),
        name="rwkv_pre",
    )(ua, prev.reshape(bsz, 1, cols), row(mu), w2, gup, row(w0), row(a0), row(k_k), row(k_a), row(r_k))


def _rwkv_scan_body(r_ref, d_ref, k_ref, v_ref, kk_ref, b_ref, s0_ref, o_ref, sfin_ref, s_ref, p2_ref, *, nbt, tc):
    c = pl.program_id(1)

    @pl.when(c == 0)
    def _():
        s_ref[...] = s0_ref[...]

    sub = min(SUBLANES, tc)
    npair = nbt * A_PAIRS
    ones = _block_ones(BF16)
    ones2 = jnp.concatenate([ones, ones], axis=0)
    ones4 = (lax.broadcasted_iota(jnp.int32, (2 * LANES, 2 * LANES), 0) // A_HEAD
             == lax.broadcasted_iota(jnp.int32, (2 * LANES, 2 * LANES), 1) // A_HEAD).astype(BF16)
    eye2 =(lax.broadcasted_iota(jnp.int32, (A_HEAD, LANES), 0)
            == lax.broadcasted_iota(jnp.int32, (A_HEAD, LANES), 1) % A_HEAD)
    eye2b = jnp.where(eye2, 1.0, 0.0).astype(BF16)
    wr = lax.broadcasted_iota(jnp.int32, (2 * sub, sub * LANES), 0)
    wc = lax.broadcasted_iota(jnp.int32, (2 * sub, sub * LANES), 1)
    wsel = jnp.where((wc // LANES == wr % sub) & ((wc % LANES) // A_HEAD == wr // sub), 1.0, 0.0).astype(BF16)
    bc = lambda x: jnp.broadcast_to(x, (A_HEAD, LANES))
    prs = [divmod(q, A_PAIRS) for q in range(npair)]

    nhalf = 2 if npair % 4 == 0 else 1
    per = npair // nhalf

    def group(gi, carry):
        rows = pl.ds(pl.multiple_of(gi * sub, sub), sub)
        tile = lambda ref, q: ref[prs[q][0], rows, pl.ds(prs[q][1] * LANES, LANES)]
        row = lambda x8, j: x8[j:j + 1, :]

        def issue(half, j):
            p1s, p3s = [], []
            for q in range(half * per, (half + 1) * per):
                nb, p = prs[q]
                v8 = tile(v_ref, q)
                v_hi8 = v8.astype(BF16)
                v_lo8 = (v8 - v_hi8.astype(F32)).astype(BF16)
                p1s.append((s_ref[nb, p] * bc(row(tile(kk_ref, q), j))).astype(BF16))
                p3s.append(jnp.concatenate([eye2b * bc(row(v_hi8, j)), eye2b * bc(row(v_lo8, j))], axis=1))
            if per % 2 == 0:
                lhs = jnp.concatenate([jnp.concatenate(p1s[i:i + 2], axis=1) for i in range(0, per, 2)], axis=0)
                sa = jnp.dot(lhs, ones4, preferred_element_type=F32)
                sas = [sa[(i // 2) * A_HEAD:(i // 2 + 1) * A_HEAD, (i % 2) * LANES:(i % 2 + 1) * LANES]
                       for i in range(per)]
            else:
                sa = jnp.dot(jnp.concatenate(p1s, axis=0), ones, preferred_element_type=F32)
                sas = [sa[i * A_HEAD:(i + 1) * A_HEAD] for i in range(per)]
            vc = jnp.dot(jnp.concatenate(p3s, axis=0), ones2, preferred_element_type=F32)
            return sas, [vc[i * A_HEAD:(i + 1) * A_HEAD] for i in range(per)]

        def update(half, j, sas, vcs):
            for i, q in enumerate(range(half * per, (half + 1) * per)):
                nb, p = prs[q]
                s = (s_ref[nb, p] * bc(row(tile(d_ref, q), j)) - sas[i] * bc(row(tile(b_ref, q), j))
                     + vcs[i] * bc(row(tile(k_ref, q), j)))
                s_ref[nb, p] = s
                p2_ref[q * A_HEAD:(q + 1) * A_HEAD, j * LANES:(j + 1) * LANES] = (
                    s * bc(row(tile(r_ref, q), j))).astype(BF16)

        pend = [issue(0, 0)] + [None] * (nhalf - 1)
        for j in range(sub):
            for half in range(1, nhalf):
                pend[half] = issue(half, j)
            update(0, j, *pend[0])
            if j + 1 < sub:
                pend[0] = issue(0, j + 1)
            for half in range(1, nhalf):
                update(half, j, *pend[half])
        for q, (nb, p) in enumerate(prs):
            qs = slice(q * A_HEAD, (q + 1) * A_HEAD)
            ot = lax.dot_general(wsel, p2_ref[qs, :], _NT, preferred_element_type=F32)
            o_ref[nb, rows, pl.ds(p * LANES, LANES)] = jnp.concatenate([ot[0:sub], ot[sub:2 * sub]], axis=1)
        return carry

    lax.fori_loop(0, tc // sub, group, 0)

    @pl.when(c == pl.num_programs(1) - 1)
    def _():
        sfin_ref[...] = s_ref[...]


def rwkv_scan(r, d, k, v, kk, b, s0, tc=64, nbt=4):
    bsz, seq_len, w = r.shape
    tc = min(tc, seq_len)
    nbt = min(nbt, bsz)
    assert seq_len % tc == 0 and bsz % nbt == 0 and w == A_WIDTH
    tok = pl.BlockSpec((nbt, tc, w), lambda i, c: (i, c, 0))
    st = pl.BlockSpec((nbt, A_PAIRS, A_HEAD, LANES), lambda i, c: (i, 0, 0, 0))
    return pl.pallas_call(
        functools.partial(_rwkv_scan_body, nbt=nbt, tc=tc),
        grid=(bsz // nbt, seq_len // tc),
        in_specs=[tok] * 6 + [st],
        out_specs=[tok, st],
        out_shape=[jax.ShapeDtypeStruct((bsz, seq_len, w), F32),
                   jax.ShapeDtypeStruct((bsz, A_PAIRS, A_HEAD, LANES), F32)],
        scratch_shapes=[pltpu.VMEM((nbt, A_PAIRS, A_HEAD, LANES), F32),
                        pltpu.VMEM((nbt * A_PAIRS * A_HEAD, min(SUBLANES, tc) * LANES), BF16)],
        compiler_params=_cparams(("parallel", "arbitrary")),
        name="rwkv_scan",
    )(r, d, k, v, kk, b, s0)


def _rwkv_post_body(o_ref, bonus_ref, g_ref, lnw_ref, lnb_ref, out_ref):
    o = o_ref[...]
    ones = _block_ones(F32)
    inv_n = 1.0 / A_HEAD
    mean = _seg_sum(o, ones) * inv_n
    cen = o - mean
    var = _seg_sum(cen * cen, ones) * inv_n
    y = cen * lax.rsqrt(var + A_GN_EPS) * lnw_ref[...] + lnb_ref[...]
    out_ref[...] = ((y + bonus_ref[...]) * g_ref[...]).astype(out_ref.dtype)


def rwkv_post(o, bonus, g, ln_w, ln_b, tr=256):
    t, w = o.shape
    tr = min(tr, t)
    tok = pl.BlockSpec((tr, w), lambda i: (i, 0))
    vec = pl.BlockSpec((1, w), lambda i: (0, 0))
    return pl.pallas_call(
        _rwkv_post_body,
        grid=(pl.cdiv(t, tr),),
        in_specs=[tok, tok, tok, vec, vec],
        out_specs=tok,
        out_shape=jax.ShapeDtypeStruct((t, w), BF16),
        compiler_params=_cparams(("parallel",)),
        name="rwkv_post",
    )(o, bonus, g, ln_w.reshape(1, w), ln_b.reshape(1, w))


def _rope_body(xa_ref, xb_ref, cos_ref, sa_ref, sb_ref, q_ref, k_ref, v_ref, *, shift, nblk):
    cos = cos_ref[...]
    sin_a = sa_ref[...]
    sin_b = sb_ref[...]
    half = B_ROT // 2
    nslab = B_WIDTH // LANES
    lane = lax.broadcasted_iota(jnp.int32, cos.shape, 1)

    def block(i):
        ref, i = (xa_ref, i) if i < nblk else (xb_ref, i - nblk)
        x = ref[0, :, i * LANES:(i + 1) * LANES]
        return pltpu.roll(x, LANES - shift, 1) if shift else x

    nxt = block(0)
    for s in range(3 * nslab):
        if shift:
            cur, nxt = nxt, block(s + 1)
            x = jnp.where(lane < LANES - shift, cur, nxt)
        else:
            x = block(s)
        if s < 2 * nslab:
            x = x * cos + pltpu.roll(x, half, 1) * sin_a + pltpu.roll(x, LANES - half, 1) * sin_b
        if s < nslab:
            q_ref[0, :, s * LANES:(s + 1) * LANES] = (x * (B_QK ** -0.5)).astype(q_ref.dtype)
        elif s < 2 * nslab:
            k_ref[0, :, (s - nslab) * LANES:(s - nslab + 1) * LANES] = x
        else:
            v_ref[0, :, (s - 2 * nslab) * LANES:(s - 2 * nslab + 1) * LANES] = x


def rope_tables(pos):
    half = B_ROT // 2
    inv = ROPE_THETA ** (-jnp.arange(half, dtype=F32) * 2.0 / B_ROT)
    ang = pos.astype(F32)[:, None] * inv[None, :]
    cos, sin = jnp.cos(ang), jnp.sin(ang)
    n = pos.shape[0]
    one = jnp.ones((n, B_QK - B_ROT), F32)
    zero = jnp.zeros((n, B_QK - B_ROT), F32)
    zh = jnp.zeros((n, half), F32)
    comp = lambda a, b_, rest: jnp.concatenate([a, b_, rest], axis=1)
    cos_p = comp(cos, cos, one)
    sin_a = comp(zh, sin, zero)
    sin_b = comp(-sin, zh, zero)
    dup = lambda x: jnp.concatenate([x, x], axis=1)
    return dup(cos_p), dup(sin_a), dup(sin_b)


def rope_qkv(u, col0, tables, tm=256):
    bsz, seq_len, width = u.shape
    tm = min(tm, seq_len)
    if col0 == 0:
        nblk, first, shift = 3 * B_WIDTH // LANES, 0, 0
    else:
        nblk, first, shift = col0 // LANES, 1, col0 % LANES
        assert 2 * nblk * LANES >= shift + 3 * B_WIDTH + LANES and 3 * nblk * LANES <= width
    win = nblk * LANES
    tab = pl.BlockSpec((tm, LANES), lambda b, j: (j, 0))
    out_spec = pl.BlockSpec((1, tm, B_WIDTH), lambda b, j: (b, j, 0))
    sds = lambda dt: jax.ShapeDtypeStruct((bsz, seq_len, B_WIDTH), dt)
    return pl.pallas_call(
        functools.partial(_rope_body, shift=shift, nblk=nblk),
        grid=(bsz, seq_len // tm),
        in_specs=[pl.BlockSpec((1, tm, win), lambda b, j: (b, j, first)),
                  pl.BlockSpec((1, tm, win), lambda b, j: (b, j, first + 1 if shift else first)),
                  tab, tab, tab],
        out_specs=[out_spec] * 3,
        out_shape=[sds(BF16), sds(F32), sds(F32)],
        compiler_params=_cparams(("parallel", "parallel")),
        name="rope_qkv",
    )(u, u, *tables)


def _lam(lqk_ref, lam_init):
    d1 = jnp.sum(lqk_ref[0:1, :] * lqk_ref[1:2, :], axis=-1, keepdims=True)
    d2 = jnp.sum(lqk_ref[2:3, :] * lqk_ref[3:4, :], axis=-1, keepdims=True)
    return jnp.exp(d1) - jnp.exp(d2) + lam_init


def _diff_finish(acc1, l1, acc2, l2, lam, subln, lam_init):
    a = acc1 / l1 - lam * (acc2 / l2)
    y = a * lax.rsqrt(jnp.mean(a * a, axis=-1, keepdims=True) + EPS) * subln
    return y * (1.0 - lam_init)


_NT = (((1,), (1,)), ((), ()))


def _attn_prompt_body(q_ref, k_ref, v_ref, lqk_ref, sub_ref, o_ref, qs_ref, m_ref, l_ref, acc_ref,
                      *, tq, tk, hp, lam_init):
    qi = pl.program_id(2)
    ki = pl.program_id(3)
    nchain = 2 * hp
    head = lambda h: slice(h * B_VDIM, (h + 1) * B_VDIM)

    @pl.when(ki == 0)
    def _():
        lane = lax.broadcasted_iota(jnp.int32, (tq, B_VDIM), 1)
        for h in range(hp):
            qh = q_ref[0, :, head(h)]
            qs_ref[2 * h] = jnp.where(lane < B_QK, qh, jnp.zeros_like(qh))
            qs_ref[2 * h + 1] = jnp.where(lane >= B_QK, qh, jnp.zeros_like(qh))
        m_ref[...] = jnp.full(m_ref.shape, NEG_INF, F32)
        l_ref[...] = jnp.zeros(l_ref.shape, F32)
        acc_ref[...] = jnp.zeros(acc_ref.shape, F32)

    def block(masked):
        ks = [k_ref[0, :, head(h)].astype(BF16) for h in range(hp)]
        vs = [v_ref[0, :, head(h)].astype(BF16) for h in range(hp)]
        ss = [lax.dot_general(qs_ref[c], ks[c // 2], _NT, preferred_element_type=F32) for c in range(nchain)]
        if masked:
            row = qi * tq + lax.broadcasted_iota(jnp.int32, (tq, tk), 0)
            col = ki * tk + lax.broadcasted_iota(jnp.int32, (tq, tk), 1)
            keep = col <= row
            ss = [jnp.where(keep, s, NEG_INF) for s in ss]
        alphas, ps = [], []
        for c, s in enumerate(ss):
            m_prev = m_ref[c]
            m_new = jnp.maximum(m_prev, jnp.max(s, axis=-1, keepdims=True))
            alpha = jnp.exp(m_prev - m_new)
            p = jnp.exp(s - m_new)
            l_ref[c] = alpha * l_ref[c] + jnp.sum(p, axis=-1, keepdims=True)
            m_ref[c] = m_new
            alphas.append(alpha)
            ps.append(p.astype(BF16))
        for c in range(nchain):
            acc_ref[c] = alphas[c] * acc_ref[c] + jnp.dot(ps[c], vs[c // 2], preferred_element_type=F32)

    visible = ki * tk < (qi + 1) * tq
    full = ki * tk + tk - 1 <= qi * tq

    @pl.when(full)
    def _():
        block(False)

    @pl.when(visible & jnp.logical_not(full))
    def _():
        block(True)

    @pl.when(ki == pl.num_programs(3) - 1)
    def _():
        lam = _lam(lqk_ref, lam_init)
        for h in range(hp):
            y = _diff_finish(acc_ref[2 * h], l_ref[2 * h], acc_ref[2 * h + 1], l_ref[2 * h + 1], lam,
                             sub_ref[...], lam_init)
            o_ref[0, :, head(h)] = y.astype(o_ref.dtype)


def attn_prompt(q, k, v, lqk, subln, lam_init, tq=256, tk=1024, hp=2):
    bsz, seq_len, _ = q.shape
    tq = min(tq, seq_len)
    tk = min(tk, seq_len)
    assert seq_len % tq == 0 and seq_len % tk == 0 and B_HEADS % hp == 0
    nq, nk = seq_len // tq, seq_len // tk
    ngrp = B_HEADS // hp
    kv_blk = lambda qi, ki: jnp.minimum(ki, ((qi + 1) * tq - 1) // tk)
    wide = hp * B_VDIM
    return pl.pallas_call(
        functools.partial(_attn_prompt_body, tq=tq, tk=tk, hp=hp, lam_init=lam_init),
        grid=(bsz, ngrp, nq, nk),
        in_specs=[
            pl.BlockSpec((1, tq, wide), lambda b, g, qi, ki: (b, qi, g)),
            pl.BlockSpec((1, tk, wide), lambda b, g, qi, ki: (b, kv_blk(qi, ki), g)),
            pl.BlockSpec((1, tk, wide), lambda b, g, qi, ki: (b, kv_blk(qi, ki), g)),
            pl.BlockSpec((4, B_QK), lambda b, g, qi, ki: (0, 0)),
            pl.BlockSpec((1, B_VDIM), lambda b, g, qi, ki: (0, 0)),
        ],
        out_specs=pl.BlockSpec((1, tq, wide), lambda b, g, qi, ki: (b, qi, g)),
        out_shape=jax.ShapeDtypeStruct((bsz, seq_len, B_WIDTH), BF16),
        scratch_shapes=[pltpu.VMEM((2 * hp, tq, B_VDIM), BF16), pltpu.VMEM((2 * hp, tq, 1), F32),
                        pltpu.VMEM((2 * hp, tq, 1), F32), pltpu.VMEM((2 * hp, tq, B_VDIM), F32)],
        compiler_params=_cparams(("parallel", "parallel", "parallel", "arbitrary")),
        name="attn_prompt",
    )(q, k, v, lqk, subln.reshape(1, B_VDIM))


def _attn_decode_body(pt_ref, q_ref, kn_ref, vn_ref, ck_ref, cv_ref, lqk_ref, sub_ref, o_ref,
                      qa_ref, m_ref, l_ref, acc_ref, bias_ref, *, nq, n_pages, lam_init):
    del pt_ref
    p = pl.program_id(1)
    rows = 2 * nq
    nrow = B_HEADS * rows
    bf_round = lambda x: x.astype(BF16).astype(F32)

    @pl.when(p == 0)
    def _():
        q = q_ref[0].astype(F32)
        lane = lax.broadcasted_iota(jnp.int32, (nq, B_VDIM), 1)
        for h in range(B_HEADS):
            qh = q[:, h * B_VDIM:(h + 1) * B_VDIM]
            qa_ref[h * rows:(h + 1) * rows, :] = jnp.concatenate(
                [jnp.where(lane < B_QK, qh, 0.0), jnp.where(lane >= B_QK, qh, 0.0)], axis=0)
        m_ref[...] = jnp.full(m_ref.shape, NEG_INF, F32)
        l_ref[...] = jnp.zeros(l_ref.shape, F32)
        acc_ref[...] = jnp.zeros(acc_ref.shape, F32)
        own = (lax.broadcasted_iota(jnp.int32, bias_ref.shape, 0) // rows
               == lax.broadcasted_iota(jnp.int32, bias_ref.shape, 1) % B_HEADS)
        bias_ref[...] = jnp.where(own, 0.0, NEG_INF)

    @pl.when(p < n_pages)
    def _():
        page = ck_ref.shape[2]
        ncol = page * B_HEADS
        k = ck_ref[0, 0].reshape(ncol, B_VDIM).astype(BF16)
        v = cv_ref[0, 0].reshape(ncol, B_VDIM).astype(BF16)
        s = lax.dot_general(qa_ref[...].astype(BF16), k, _NT, preferred_element_type=F32) + bias_ref[...]
        m_prev = m_ref[...]
        m_new = jnp.maximum(m_prev, jnp.max(s, axis=-1, keepdims=True))
        alpha = jnp.exp(m_prev - m_new)
        pr = jnp.exp(s - m_new)
        l_ref[...] = alpha * l_ref[...] + jnp.sum(pr, axis=-1, keepdims=True)
        acc_ref[...] = alpha * acc_ref[...] + jnp.dot(pr.astype(BF16), v, preferred_element_type=F32)
        m_ref[...] = m_new

    @pl.when(p == n_pages)
    def _():
        lam = _lam(lqk_ref, lam_init)
        qrow = lax.broadcasted_iota(jnp.int32, (nrow, 1), 0) % nq
        qa = bf_round(qa_ref[...])
        kn = bf_round(kn_ref[0])
        vn = bf_round(vn_ref[0])
        per_head = lambda x, j: jnp.concatenate(
            [jnp.broadcast_to(x[j:j + 1, h * B_VDIM:(h + 1) * B_VDIM], (rows, B_VDIM)) for h in range(B_HEADS)], axis=0)
        ss = [jnp.where(qrow >= j, jnp.sum(qa * per_head(kn, j), axis=-1, keepdims=True), NEG_INF)
              for j in range(nq)]
        m_prev = m_ref[...]
        m_new = m_prev
        for sj in ss:
            m_new = jnp.maximum(m_new, sj)
        alpha = jnp.exp(m_prev - m_new)
        l = alpha * l_ref[...]
        acc = alpha * acc_ref[...]
        for j, sj in enumerate(ss):
            pj = jnp.exp(sj - m_new)
            l = l + pj
            acc = acc + bf_round(pj) * per_head(vn, j)
        n = acc / l
        a = n - lam * pltpu.roll(n, nrow - nq, 0)
        y = a * lax.rsqrt(jnp.mean(a * a, axis=-1, keepdims=True) + EPS) * sub_ref[...] * (1.0 - lam_init)
        for h in range(B_HEADS):
            o_ref[0, :, h * B_VDIM:(h + 1) * B_VDIM] = y[h * rows:h * rows + nq].astype(o_ref.dtype)


def attn_decode(q, k, v, cache_k, cache_v, page_table, layer, lqk, subln, lam_init):
    bsz, nq, _ = q.shape
    n_pages = page_table.shape[1]
    page = cache_k.shape[2]
    nrow = B_HEADS * 2 * nq
    cache_spec = pl.BlockSpec(
        (1, 1, page, B_HEADS, B_VDIM),
        lambda b, p, pt: (layer, pt[b * n_pages + jnp.minimum(p, n_pages - 1)], 0, 0, 0))
    grid_spec = pltpu.PrefetchScalarGridSpec(
        num_scalar_prefetch=1,
        grid=(bsz, n_pages + 1),
        in_specs=[
            pl.BlockSpec((1, nq, B_WIDTH), lambda b, p, pt: (b, 0, 0)),
            pl.BlockSpec((1, nq, B_WIDTH), lambda b, p, pt: (b, 0, 0)),
            pl.BlockSpec((1, nq, B_WIDTH), lambda b, p, pt: (b, 0, 0)),
            cache_spec, cache_spec,
            pl.BlockSpec((4, B_QK), lambda b, p, pt: (0, 0)),
            pl.BlockSpec((1, B_VDIM), lambda b, p, pt: (0, 0)),
        ],
        out_specs=pl.BlockSpec((1, nq, B_WIDTH), lambda b, p, pt: (b, 0, 0)),
        scratch_shapes=[
            pltpu.VMEM((nrow, B_VDIM), F32),
            pltpu.VMEM((nrow, 1), F32),
            pltpu.VMEM((nrow, 1), F32),
            pltpu.VMEM((nrow, B_VDIM), F32),
            pltpu.VMEM((nrow, page * B_HEADS), F32),
        ],
    )
    return pl.pallas_call(
        functools.partial(_attn_decode_body, nq=nq, n_pages=n_pages, lam_init=lam_init),
        grid_spec=grid_spec,
        out_shape=jax.ShapeDtypeStruct((bsz, nq, B_WIDTH), F32),
        compiler_params=_cparams(("parallel", "arbitrary")),
        name="attn_decode",
    )(page_table.reshape(-1), q, k, v, cache_k, cache_v, lqk, subln.reshape(1, B_VDIM)).astype(BF16)


_TN = (((0,), (0,)), ((), ()))


def _hgrn_body(qa_ref, qb_ref, fa_ref, fb_ref, ia_ref, ib_ref, ga_ref, gb_ref, lb_ref, cn_ref, s0_ref,
               o_ref, sfin_ref, s_ref, *, tc, n_valid, shift):
    c = pl.program_id(2)
    lane_in = lax.broadcasted_iota(jnp.int32, (tc, LANES), 1)

    def slab(a_ref, b_ref):
        if not shift:
            return a_ref[0]
        return jnp.where(lane_in < LANES - shift, pltpu.roll(a_ref[0], LANES - shift, 1),
                         pltpu.roll(b_ref[0], LANES - shift, 1))

    @pl.when(c == 0)
    def _():
        s_ref[...] = s0_ref[0, 0]

    nrow = -(-tc // C_CHUNK) * C_CHUNK
    pad = lambda x: x if nrow == tc else jnp.concatenate([x, jnp.zeros((nrow - tc, x.shape[1]), F32)], axis=0)
    q = _silu(pad(slab(qa_ref, qb_ref)))
    lb = lb_ref[...]
    fs = lb + (1.0 - lb) * _sigmoid(pad(slab(fa_ref, fb_ref)))
    kin = 1.0 - fs
    gl = jnp.log(fs)
    iv = pad(slab(ia_ref, ib_ref))
    row = lax.broadcasted_iota(jnp.int32, (nrow, C_DK), 0)
    if n_valid < nrow:
        kin = jnp.where(row < n_valid, kin, 0.0)
        gl = jnp.where(row < n_valid, gl, 0.0)
    b = gl
    sh = 1
    while sh < C_CHUNK:
        b = b + jnp.where(row % C_CHUNK >= sh, pltpu.roll(b, sh, 0), 0.0)
        sh *= 2
    trow = lax.broadcasted_iota(jnp.int32, (C_CHUNK, C_DK), 0)
    eye = (lax.broadcasted_iota(jnp.int32, (C_DK, C_DK), 0)
           == lax.broadcasted_iota(jnp.int32, (C_DK, C_DK), 1)).astype(F32)
    outs = []
    for ch in range(nrow // C_CHUNK):
        rs = slice(ch * C_CHUNK, (ch + 1) * C_CHUNK)
        bc, qc, kc, ic = b[rs], q[rs], kin[rs], iv[rs]
        s = s_ref[...]
        o = jnp.dot((qc * jnp.exp(bc)).astype(BF16), s.astype(BF16), preferred_element_type=F32)
        for sidx in range(C_CHUNK):
            dec = jnp.exp(jnp.where(trow >= sidx, bc - bc[sidx:sidx + 1, :], NEG_INF))
            att = jnp.sum(qc * kc[sidx:sidx + 1, :] * dec, axis=-1, keepdims=True)
            o = o + att * ic[sidx:sidx + 1, :]
        bl = bc[C_CHUNK - 1:C_CHUNK, :]
        kd = kc * jnp.exp(bl - bc)
        inc = lax.dot_general(kd.astype(BF16), ic.astype(BF16), _TN, preferred_element_type=F32)
        ecol = jnp.sum(eye * jnp.exp(bl), axis=1, keepdims=True)
        s_ref[...] = s * ecol + inc
        outs.append(o)
    o = jnp.concatenate(outs, axis=0)[0:tc]
    y = o * lax.rsqrt(jnp.mean(o * o, axis=-1, keepdims=True) + EPS) * cn_ref[...]
    o_ref[0] = (y * _silu(slab(ga_ref, gb_ref))).astype(o_ref.dtype)

    @pl.when(c == pl.num_programs(2) - 1)
    def _():
        sfin_ref[0, 0] = s_ref[...]


def hgrn(u, col0, s0, lb, c_norm, tc=128):
    bsz, seq_len, _ = u.shape
    tc = min(tc, seq_len)
    assert seq_len % tc == 0
    blk0, shift = col0 // LANES, col0 % LANES
    tok = lambda off: pl.BlockSpec((1, tc, LANES), lambda b, h, c: (b, c, blk0 + off + h))
    pair = lambda grp: [tok(grp * C_HEADS), tok(grp * C_HEADS + (1 if shift else 0))]
    st = pl.BlockSpec((1, 1, C_DK, C_DV), lambda b, h, c: (b, h, 0, 0))
    return pl.pallas_call(
        functools.partial(_hgrn_body, tc=tc, n_valid=tc, shift=shift),
        grid=(bsz, C_HEADS, seq_len // tc),
        in_specs=pair(0) + pair(1) + pair(2) + pair(3) + [
            pl.BlockSpec((1, LANES), lambda b, h, c: (0, h)),
            pl.BlockSpec((1, C_DV), lambda b, h, c: (0, 0)),
            st],
        out_specs=[pl.BlockSpec((1, tc, LANES), lambda b, h, c: (b, c, h)), st],
        out_shape=[jax.ShapeDtypeStruct((bsz, seq_len, C_WIDTH), BF16),
                   jax.ShapeDtypeStruct((bsz, C_HEADS, C_DK, C_DV), F32)],
        scratch_shapes=[pltpu.VMEM((C_DK, C_DV), F32)],
        compiler_params=_cparams(("parallel", "parallel", "arbitrary")),
        name="hgrn",
    )(*([u] * 8), lb.reshape(1, C_FK), c_norm.reshape(1, C_DV), s0)


def _pair_state(s):
    bsz = s.shape[0]
    return s.reshape(bsz, A_PAIRS, 2, A_HEAD, A_HEAD).transpose(0, 1, 3, 2, 4).reshape(bsz, A_PAIRS, A_HEAD, LANES)


def _unpair_state(s):
    bsz = s.shape[0]
    return s.reshape(bsz, A_PAIRS, A_HEAD, 2, A_HEAD).transpose(0, 1, 3, 2, 4).reshape(bsz, A_HEADS, A_HEAD, A_HEAD)


def _layer_params(l, P):
    padc = A_COLS_PAD - A_COLS
    w2 = jnp.zeros((LANES, 2 * A_WIDTH), F32)
    w2 = w2.at[0:A_DECAY_R, 0:A_WIDTH].set(P["a_w_up"][l])
    w2 = w2.at[A_DECAY_R:A_DECAY_R + A_ICLR_R, A_WIDTH:].set(P["a_a_up"][l])
    return dict(
        mu=jnp.pad(P["a_mu"][l], (0, padc)),
        w2=w2,
        gup=jnp.pad(P["a_g_up"][l], ((0, 2 * LANES - A_GATE_R), (0, 0))),
        lqk=jnp.stack([P["b_lq1"][l], P["b_lk1"][l], P["b_lq2"][l], P["b_lk2"][l]]),
        r_k=P["a_r_k"][l].reshape(A_WIDTH),
    )


def _run_trunk(x, pos, caches, wkv0, shift0, hgrn0, conv0, lbs, P, LP):
    bsz, seq_len, d = x.shape
    t = bsz * seq_len
    h = x.reshape(t, d)
    tables = rope_tables(pos)
    padc = A_COLS_PAD - A_COLS
    ks, vs, wkvs, shifts, hgrns, convs = [], [], [], [], [], []
    for l in range(DEPTH):
        lp = LP[l]
        hn = rmsnorm_rows(h, P["norm_mix"][l], BF16)
        u = matmul(hn, P["w_in"], layer=l, pad_n=True)
        u = u.reshape(bsz, seq_len, u.shape[-1])
        r, dcy, kp, v, kk, bb, g, bonus = rwkv_pre(
            u, jnp.pad(shift0[l], ((0, 0), (0, padc))), lp["mu"], lp["w2"], lp["gup"],
            P["a_w0"][l], P["a_a0"][l], P["a_k_k"][l], P["a_k_a"][l], lp["r_k"])
        o_a, s_a = rwkv_scan(r, dcy, kp, v, kk, bb, _pair_state(wkv0[l]))
        flat = lambda z: z.reshape(t, A_WIDTH)
        oa = rwkv_post(flat(o_a), flat(bonus), flat(g), P["a_ln_w"][l], P["a_ln_b"][l])
        lam_init = 0.8 - 0.6 * math.exp(-0.3 * l)
        qb, kb, vb = rope_qkv(u, A_COLS, tables)
        if caches is None:
            ob = attn_prompt(qb, kb, vb, lp["lqk"], P["b_subln"][l], lam_init)
        else:
            ob = attn_decode(qb, kb, vb, caches[0], caches[1], caches[2], l, lp["lqk"], P["b_subln"][l], lam_init)
        oc, s_c = hgrn(u, A_COLS + 3 * B_WIDTH, hgrn0[l], lbs[l], P["c_norm"][l])
        mix = jnp.concatenate([oa, ob.reshape(t, B_WIDTH), oc.reshape(t, C_WIDTH)], axis=-1)
        h = matmul(mix, P["w_out"], layer=l, res=h)
        hf = rmsnorm_rows(h, P["norm_ffn"][l], BF16)
        mid, buf = ffn_up(hf, P["f_gate"], P["f_up"], P["f_conv"], P["f_conv_b"], conv0[l], seq_len, layer=l)
        h = matmul(mid, P["f_down"], layer=l, res=h, tn=1024, tk=1024)
        ks.append(kb.reshape(bsz, seq_len, B_HEADS, B_VDIM))
        vs.append(vb.reshape(bsz, seq_len, B_HEADS, B_VDIM))
        wkvs.append(_unpair_state(s_a))
        shifts.append(u[:, -1, :A_COLS])
        hgrns.append(s_c)
        convs.append(buf)
    y = rmsnorm_rows(h, P["norm_final"], F32).reshape(bsz, seq_len, d)
    st = jnp.stack
    return y, st(ks), st(vs), st(wkvs), st(shifts), st(hgrns), st(convs)


def kernel(x_prompt, x_sample, cache_k, cache_v, page_table, state_wkv, state_shift, state_hgrn, state_conv,
           norm_mix, w_in, a_mu, a_w0, a_w_up, a_a0, a_a_up, a_g_up, a_k_k, a_k_a, a_r_k, a_ln_w, a_ln_b,
           b_lq1, b_lk1, b_lq2, b_lk2, b_subln, c_lb_logits, c_norm, w_out, norm_ffn, f_gate, f_up,
           f_conv, f_conv_b, f_down, norm_final):
    P = dict(norm_mix=norm_mix, w_in=w_in, a_mu=a_mu, a_w0=a_w0, a_w_up=a_w_up, a_a0=a_a0, a_a_up=a_a_up,
             a_g_up=a_g_up, a_k_k=a_k_k, a_k_a=a_k_a, a_r_k=a_r_k, a_ln_w=a_ln_w, a_ln_b=a_ln_b,
             b_lq1=b_lq1, b_lk1=b_lk1, b_lq2=b_lq2, b_lk2=b_lk2, b_subln=b_subln, c_norm=c_norm,
             w_out=w_out, norm_ffn=norm_ffn, f_gate=f_gate, f_up=f_up, f_conv=f_conv, f_conv_b=f_conv_b,
             f_down=f_down, norm_final=norm_final)
    LP = [_layer_params(l, P) for l in range(DEPTH)]
    lbs = lower_bounds(c_lb_logits)
    bp, lp_ = x_prompt.shape[:2]
    past_len = page_table.shape[1] * cache_k.shape[2]
    pos_p = jnp.arange(lp_)
    pos_s = past_len + jnp.arange(x_sample.shape[1])
    zeros = lambda *s: jnp.zeros(s, F32)
    out_p = _run_trunk(
        x_prompt, pos_p, None,
        zeros(DEPTH, bp, A_HEADS, A_HEAD, A_HEAD), zeros(DEPTH, bp, A_COLS),
        zeros(DEPTH, bp, C_HEADS, C_DK, C_DV), zeros(DEPTH, bp, CONV_W - 1, D_FF), lbs, P, LP)
    caches = (cache_k, cache_v, page_table)
    out_s = _run_trunk(x_sample, pos_s, caches, state_wkv, state_shift, state_hgrn, state_conv, lbs, P, LP)
    y_p, k_p, v_p, wkv_p, sh_p, hg_p, cv_p = out_p
    y_s, k_s, v_s, wkv_s, sh_s, hg_s, cv_s = out_s
    return (y_p, y_s, k_p, v_p, k_s, v_s, wkv_p, wkv_s, sh_p, sh_s, hg_p, hg_s, cv_p, cv_s)
```

```python
import functools
import math

import jax
import jax.numpy as jnp
from jax import lax
from jax.experimental import pallas as pl
from jax.experimental.pallas import tpu as pltpu

F32 = jnp.float32
BF16 = jnp.bfloat16

V7X_VMEM_BYTES = 64 * 1024 * 1024
VMEM_LIMIT = V7X_VMEM_BYTES - 8 * 1024 * 1024
LANES = 128
SUBLANES = 8

D_MODEL = 4096
DEPTH = 4
A_WIDTH = D_MODEL // 4
A_HEAD = 64
A_HEADS = A_WIDTH // A_HEAD
A_PAIRS = A_HEADS // 2
A_DECAY_R = 64
A_ICLR_R = 64
A_GATE_R = 160
A_COLS = 3 * A_WIDTH + A_DECAY_R + A_ICLR_R + A_GATE_R
A_COLS_PAD = 3 * A_WIDTH + 3 * LANES
A_GN_EPS = 64e-5
B_WIDTH = D_MODEL // 2
B_HEADS = 16
B_VDIM = B_WIDTH // B_HEADS
B_QK = B_VDIM // 2
B_ROT = B_QK // 4
ROPE_THETA = 500000.0
C_WIDTH = D_MODEL - A_WIDTH - B_WIDTH
C_HEADS = 8
C_DK = 128
C_DV = C_WIDTH // C_HEADS
C_FK = C_HEADS * C_DK
C_CHUNK = 16
D_FF = 11008
CONV_W = 3
EPS = 1e-5
NEG_INF = float("-inf")


def _cparams(sem):
    return pltpu.CompilerParams(dimension_semantics=sem, vmem_limit_bytes=VMEM_LIMIT)


def _sigmoid(x):
    return 1.0 / (1.0 + jnp.exp(-x))


def _silu(x):
    return x * _sigmoid(x)


def _block_ones(dtype):
    r = lax.broadcasted_iota(jnp.int32, (LANES, LANES), 0) // A_HEAD
    c = lax.broadcasted_iota(jnp.int32, (LANES, LANES), 1) // A_HEAD
    return (r == c).astype(dtype)


def _rmsnorm_body(x_ref, w_ref, o_ref):
    x = x_ref[...]
    y = x * lax.rsqrt(jnp.mean(x * x, axis=-1, keepdims=True) + EPS)
    o_ref[...] = (y * w_ref[...]).astype(o_ref.dtype)


def rmsnorm_rows(x, w, out_dtype, tr=256):
    t, d = x.shape
    tr = min(tr, t)
    return pl.pallas_call(
        _rmsnorm_body,
        grid=(pl.cdiv(t, tr),),
        in_specs=[pl.BlockSpec((tr, d), lambda i: (i, 0)), pl.BlockSpec((1, d), lambda i: (0, 0))],
        out_specs=pl.BlockSpec((tr, d), lambda i: (i, 0)),
        out_shape=jax.ShapeDtypeStruct((t, d), out_dtype),
        compiler_params=_cparams(("parallel",)),
        name="rmsnorm_rows",
    )(x, w.reshape(1, d))


def _mm_body(*refs, nk, tk, k_total, has_res):
    if has_res:
        a_ref, w_ref, r_ref, o_ref = refs
    else:
        a_ref, w_ref, o_ref = refs
        r_ref = None
    a = a_ref[...]
    w = w_ref[...].astype(BF16)
    if nk == 1:
        acc = jnp.dot(a, w, preferred_element_type=F32)
        if has_res:
            acc = acc + r_ref[...]
        o_ref[...] = acc.astype(o_ref.dtype)
        return
    k = pl.program_id(2)
    if k_total % tk:
        valid = k_total - k * tk
        a = jnp.where(lax.broadcasted_iota(jnp.int32, a.shape, 1) < valid, a, jnp.zeros_like(a))
        w = jnp.where(lax.broadcasted_iota(jnp.int32, w.shape, 0) < valid, w, jnp.zeros_like(w))
    part = jnp.dot(a, w, preferred_element_type=F32)

    @pl.when(k == 0)
    def _():
        o_ref[...] = (part + r_ref[...]) if has_res else part

    @pl.when(k > 0)
    def _():
        o_ref[...] += part


def matmul(a, w, layer=None, res=None, out_dtype=F32, tm=2048, tn=256, tk=None, pad_n=False):
    m, kdim = a.shape
    n = w.shape[-1]
    tm = min(tm, m)
    tk = kdim if tk is None else tk
    nk = pl.cdiv(kdim, tk)
    if nk > 1:
        assert out_dtype == F32
    grid = (pl.cdiv(m, tm), pl.cdiv(n, tn), nk)
    n_out = grid[1] * tn if pad_n else n
    once = dict(pipeline_mode=pl.Buffered(1))
    a_kw = once if (nk == 1 and grid[1] > 1) else {}
    if layer is None:
        w_spec = pl.BlockSpec((tk, tn), lambda i, j, k: (k, j))
    else:
        w_spec = pl.BlockSpec((None, tk, tn), lambda i, j, k: (layer, k, j))
    in_specs = [pl.BlockSpec((tm, tk), lambda i, j, k: (i, k), **a_kw), w_spec]
    args = [a, w]
    if res is not None:
        in_specs.append(pl.BlockSpec((tm, tn), lambda i, j, k: (i, j), **(once if nk > 1 else {})))
        args.append(res)
    return pl.pallas_call(
        functools.partial(_mm_body, nk=nk, tk=tk, k_total=kdim, has_res=res is not None),
        grid=grid,
        in_specs=in_specs,
        out_specs=pl.BlockSpec((tm, tn), lambda i, j, k: (i, j)),
        out_shape=jax.ShapeDtypeStruct((m, n_out), out_dtype),
        compiler_params=_cparams(("parallel", "parallel", "arbitrary")),
        name="matmul",
    )(*args)


def _mm_kloop_body(a_ref, w_ref, r_ref, o_ref, *, tk):
    k_total = a_ref.shape[1]
    acc = r_ref[...]
    for k0 in range(0, k_total, tk):
        kw = min(tk, k_total - k0)
        acc = acc + jnp.dot(a_ref[:, k0:k0 + kw], w_ref[k0:k0 + kw, :].astype(BF16), preferred_element_type=F32)
    o_ref[...] = acc


def matmul_kloop(a, w, layer, res, tm=1024, tn=256, tk=1024):
    m, kdim = a.shape
    n = w.shape[-1]
    tm = min(tm, m)
    assert m % tm == 0 and n % tn == 0
    return pl.pallas_call(
        functools.partial(_mm_kloop_body, tk=tk),
        grid=(m // tm, n // tn),
        in_specs=[pl.BlockSpec((tm, kdim), lambda i, j: (i, 0), pipeline_mode=pl.Buffered(1)),
                  pl.BlockSpec((None, kdim, tn), lambda i, j: (layer, 0, j)),
                  pl.BlockSpec((tm, tn), lambda i, j: (i, j))],
        out_specs=pl.BlockSpec((tm, tn), lambda i, j: (i, j)),
        out_shape=jax.ShapeDtypeStruct((m, n), F32),
        compiler_params=_cparams(("parallel", "parallel")),
        name="matmul_kloop",
    )(a, w, res)


def _ffn_up_body(a_ref, wg_ref, wu_ref, cw_ref, cb_ref, buf_ref, o_ref, nbuf_ref, ext_ref, *, nb, seq_len):
    tm = nb * seq_len
    a = a_ref[...]
    gate = jnp.dot(a, wg_ref[...].astype(BF16), preferred_element_type=F32)
    up = jnp.dot(a, wu_ref[...].astype(BF16), preferred_element_type=F32)
    ext_ref[0:SUBLANES, :] = jnp.zeros((SUBLANES, gate.shape[1]), F32)
    ext_ref[SUBLANES:SUBLANES + tm, :] = gate
    g1 = ext_ref[SUBLANES - 1:SUBLANES - 1 + tm, :]
    g2 = ext_ref[SUBLANES - 2:SUBLANES - 2 + tm, :]
    row = lax.broadcasted_iota(jnp.int32, gate.shape, 0)
    for s in range(nb):
        b0 = buf_ref[s, 0:1, :]
        b1 = buf_ref[s, 1:2, :]
        g1 = jnp.where(row == s * seq_len, b1, g1)
        g2 = jnp.where(row == s * seq_len, b0, g2)
        g2 = jnp.where(row == s * seq_len + 1, b1, g2)
        last = SUBLANES + (s + 1) * seq_len
        nbuf_ref[s, :, :] = ext_ref[last - 2:last, :]
    conv = cb_ref[...] + g2 * cw_ref[0:1, :] + g1 * cw_ref[1:2, :] + gate * cw_ref[2:3, :]
    o_ref[...] = (_silu(conv) * up).astype(o_ref.dtype)


def ffn_up(hn, w_gate, w_up, w_conv, b_conv, buf, seq_len, layer=None, tn=256):
    t, d = hn.shape
    n = w_gate.shape[-1]
    nseq = buf.shape[0]
    nb = 1 if seq_len >= 256 else nseq
    tm = nb * seq_len
    assert seq_len >= 2 and t == nseq * seq_len and n % tn == 0
    grid = (t // tm, n // tn)
    if layer is None:
        wspec = lambda rows: pl.BlockSpec((rows, tn), lambda i, j: (0, j))
        b_conv = b_conv.reshape(1, n)
    else:
        wspec = lambda rows: pl.BlockSpec((None, rows, tn), lambda i, j: (layer, 0, j))
        b_conv = b_conv.reshape(-1, 1, n)
    return pl.pallas_call(
        functools.partial(_ffn_up_body, nb=nb, seq_len=seq_len),
        grid=grid,
        in_specs=[
            pl.BlockSpec((tm, d), lambda i, j: (i, 0), pipeline_mode=pl.Buffered(1)),
            wspec(d), wspec(d), wspec(CONV_W), wspec(1),
            pl.BlockSpec((nb, CONV_W - 1, tn), lambda i, j: (i, 0, j)),
        ],
        out_specs=[
            pl.BlockSpec((tm, tn), lambda i, j: (i, j)),
            pl.BlockSpec((nb, CONV_W - 1, tn), lambda i, j: (i, 0, j)),
        ],
        out_shape=[
            jax.ShapeDtypeStruct((t, n), BF16),
            jax.ShapeDtypeStruct((nseq, CONV_W - 1, n), F32),
        ],
        scratch_shapes=[pltpu.VMEM((SUBLANES + tm, tn), F32)],
        compiler_params=_cparams(("parallel", "parallel")),
        name="ffn_up",
    )(hn, w_gate, w_up, w_conv, b_conv, buf)


def _lower_bounds_body(x_ref, o_ref):
    x = x_ref[...]
    e = jnp.exp(x - jnp.max(x, axis=0, keepdims=True))
    p = e / jnp.sum(e, axis=0, keepdims=True)
    run = jnp.zeros_like(p[0:1, :])
    for l in range(x.shape[0]):
        run = run + p[l:l + 1, :]
        o_ref[l:l + 1, :] = run - p[0:1, :]


def lower_bounds(logits):
    return pl.pallas_call(
        _lower_bounds_body,
        out_shape=jax.ShapeDtypeStruct(logits.shape, F32),
        name="lower_bounds",
    )(logits)


def _seg_sum(x, ones):
    parts = []
    for s in range(x.shape[1] // LANES):
        parts.append(jnp.dot(x[:, s * LANES:(s + 1) * LANES], ones,
                             preferred_element_type=F32, precision=lax.Precision.HIGHEST))
    return jnp.concatenate(parts, axis=1)


def _rwkv_pre_body(u_ref, prev_ref, mu_ref, w2_ref, gup_ref, w0_ref, a0_ref, kk_ref, ka_ref, rk_ref,
                   r_o, d_o, k_o, v_o, kk_o, b_o, g_o, bonus_o, ext_ref, *, tm):
    j = pl.program_id(1)
    u = u_ref[0]

    @pl.when(j == 0)
    def _():
        ext_ref[SUBLANES - 1:SUBLANES, :] = prev_ref[0]

    ext_ref[SUBLANES:SUBLANES + tm, :] = u
    shifted = ext_ref[SUBLANES - 1:SUBLANES - 1 + tm, :]
    ext_ref[SUBLANES - 1:SUBLANES, :] = u[tm - 1:tm, :]
    xm = u + (shifted - u) * mu_ref[...]
    w = A_WIDTH
    r = xm[:, 0:w]
    k = xm[:, w:2 * w]
    v = xm[:, 2 * w:3 * w]
    lr = xm[:, 3 * w:3 * w + LANES]
    gd = xm[:, 3 * w + LANES:3 * w + 3 * LANES]
    lane = lax.broadcasted_iota(jnp.int32, lr.shape, 1)
    lr = jnp.where(lane < A_DECAY_R, jnp.tanh(lr), lr)
    wa = jnp.dot(lr.astype(BF16), w2_ref[...].astype(BF16), preferred_element_type=F32)
    wraw = -(w0_ref[...] + wa[:, 0:w])
    softplus = jnp.maximum(wraw, 0.0) + jnp.log(1.0 + jnp.exp(-jnp.abs(wraw)))
    decay = jnp.exp(-jnp.exp(-softplus - 0.5))
    a = _sigmoid(a0_ref[...] + wa[:, w:2 * w])
    g = jnp.dot(_sigmoid(gd).astype(BF16), gup_ref[...].astype(BF16), preferred_element_type=F32)
    ones = _block_ones(F32)
    kk = k * kk_ref[...]
    norm = jnp.sqrt(_seg_sum(kk * kk, ones))
    kk = kk / jnp.maximum(norm, 1e-12)
    kp = k * (1.0 + (a - 1.0) * ka_ref[...])
    bonus = _seg_sum(r * kp * rk_ref[...], ones) * v
    r_o[0] = r
    d_o[0] = decay
    k_o[0] = kp
    v_o[0] = v
    kk_o[0] = kk
    b_o[0] = kk * a
    g_o[0] = g
    bonus_o[0] = bonus


def rwkv_pre(ua, prev, mu, w2, gup, w0, a0, k_k, k_a, r_k, tm=256):
    bsz, seq_len, width = ua.shape
    cols = A_COLS_PAD
    tm = min(tm, seq_len)
    assert seq_len % tm == 0 and width % cols == 0
    w = A_WIDTH
    row = lambda x: x.reshape(1, -1)
    vec = lambda n: pl.BlockSpec((1, n), lambda b, j: (0, 0))
    out_spec = pl.BlockSpec((1, tm, w), lambda b, j: (b, j, 0))
    out_sds = jax.ShapeDtypeStruct((bsz, seq_len, w), F32)
    return pl.pallas_call(
        functools.partial(_rwkv_pre_body, tm=tm),
        grid=(bsz, seq_len // tm),
        in_specs=[
            pl.BlockSpec((1, tm, cols), lambda b, j: (b, j, 0)),
            pl.BlockSpec((1, 1, cols), lambda b, j: (b, 0, 0)),
            vec(cols),
            pl.BlockSpec((LANES, 2 * w), lambda b, j: (0, 0)),
            pl.BlockSpec((2 * LANES, w), lambda b, j: (0, 0)),
            vec(w), vec(w), vec(w), vec(w), vec(w),
        ],
        out_specs=[out_spec] * 8,
        out_shape=[out_sds] * 8,
        scratch_shapes=[pltpu.VMEM((SUBLANES + tm, cols), F32)],
        compiler_params=_cparams(("parallel", "arbitrary")),
        name="rwkv_pre",
    )(ua, prev.reshape(bsz, 1, cols), row(mu), w2, gup, row(w0), row(a0), row(k_k), row(k_a), row(r_k))


def _rwkv_scan_body(r_ref, d_ref, k_ref, v_ref, kk_ref, b_ref, s0_ref, o_ref, sfin_ref, s_ref, p2_ref, *, nbt, tc):
    c = pl.program_id(1)

    @pl.when(c == 0)
    def _():
        s_ref[...] = s0_ref[...]

    sub = min(SUBLANES, tc)
    npair = nbt * A_PAIRS
    ones = _block_ones(BF16)
    ones2 = jnp.concatenate([ones, ones], axis=0)
    ones4 = (lax.broadcasted_iota(jnp.int32, (2 * LANES, 2 * LANES), 0) // A_HEAD
             == lax.broadcasted_iota(jnp.int32, (2 * LANES, 2 * LANES), 1) // A_HEAD).astype(BF16)
    eye2 =(lax.broadcasted_iota(jnp.int32, (A_HEAD, LANES), 0)
            == lax.broadcasted_iota(jnp.int32, (A_HEAD, LANES), 1) % A_HEAD)
    eye2b = jnp.where(eye2, 1.0, 0.0).astype(BF16)
    wr = lax.broadcasted_iota(jnp.int32, (2 * sub, sub * LANES), 0)
    wc = lax.broadcasted_iota(jnp.int32, (2 * sub, sub * LANES), 1)
    wsel = jnp.where((wc // LANES == wr % sub) & ((wc % LANES) // A_HEAD == wr // sub), 1.0, 0.0).astype(BF16)
    bc = lambda x: jnp.broadcast_to(x, (A_HEAD, LANES))
    prs = [divmod(q, A_PAIRS) for q in range(npair)]

    nhalf = 2 if npair % 4 == 0 else 1
    per = npair // nhalf

    def group(gi, carry):
        rows = pl.ds(pl.multiple_of(gi * sub, sub), sub)
        tile = lambda ref, q: ref[prs[q][0], rows, pl.ds(prs[q][1] * LANES, LANES)]
        row = lambda x8, j: x8[j:j + 1, :]

        def issue(half, j):
            p1s, p3s = [], []
            for q in range(half * per, (half + 1) * per):
                nb, p = prs[q]
                v8 = tile(v_ref, q)
                v_hi8 = v8.astype(BF16)
                v_lo8 = (v8 - v_hi8.astype(F32)).astype(BF16)
                p1s.append((s_ref[nb, p] * bc(row(tile(kk_ref, q), j))).astype(BF16))
                p3s.append(jnp.concatenate([eye2b * bc(row(v_hi8, j)), eye2b * bc(row(v_lo8, j))], axis=1))
            if per % 2 == 0:
                lhs = jnp.concatenate([jnp.concatenate(p1s[i:i + 2], axis=1) for i in range(0, per, 2)], axis=0)
                sa = jnp.dot(lhs, ones4, preferred_element_type=F32)
                sas = [sa[(i // 2) * A_HEAD:(i // 2 + 1) * A_HEAD, (i % 2) * LANES:(i % 2 + 1) * LANES]
                       for i in range(per)]
            else:
                sa = jnp.dot(jnp.concatenate(p1s, axis=0), ones, preferred_element_type=F32)
                sas = [sa[i * A_HEAD:(i + 1) * A_HEAD] for i in range(per)]
            vc = jnp.dot(jnp.concatenate(p3s, axis=0), ones2, preferred_element_type=F32)
            return sas, [vc[i * A_HEAD:(i + 1) * A_HEAD] for i in range(per)]

        def update(half, j, sas, vcs):
            for i, q in enumerate(range(half * per, (half + 1) * per)):
                nb, p = prs[q]
                s = (s_ref[nb, p] * bc(row(tile(d_ref, q), j)) - sas[i] * bc(row(tile(b_ref, q), j))
                     + vcs[i] * bc(row(tile(k_ref, q), j)))
                s_ref[nb, p] = s
                p2_ref[q * A_HEAD:(q + 1) * A_HEAD, j * LANES:(j + 1) * LANES] = (
                    s * bc(row(tile(r_ref, q), j))).astype(BF16)

        pend = [issue(0, 0)] + [None] * (nhalf - 1)
        for j in range(sub):
            for half in range(1, nhalf):
                pend[half] = issue(half, j)
            update(0, j, *pend[0])
            if j + 1 < sub:
                pend[0] = issue(0, j + 1)
            for half in range(1, nhalf):
                update(half, j, *pend[half])
        for q, (nb, p) in enumerate(prs):
            qs = slice(q * A_HEAD, (q + 1) * A_HEAD)
            ot = lax.dot_general(wsel, p2_ref[qs, :], _NT, preferred_element_type=F32)
            o_ref[nb, rows, pl.ds(p * LANES, LANES)] = jnp.concatenate([ot[0:sub], ot[sub:2 * sub]], axis=1)
        return carry

    lax.fori_loop(0, tc // sub, group, 0)

    @pl.when(c == pl.num_programs(1) - 1)
    def _():
        sfin_ref[...] = s_ref[...]


def rwkv_scan(r, d, k, v, kk, b, s0, tc=64, nbt=4):
    bsz, seq_len, w = r.shape
    tc = min(tc, seq_len)
    nbt = min(nbt, bsz)
    assert seq_len % tc == 0 and bsz % nbt == 0 and w == A_WIDTH
    tok = pl.BlockSpec((nbt, tc, w), lambda i, c: (i, c, 0))
    st = pl.BlockSpec((nbt, A_PAIRS, A_HEAD, LANES), lambda i, c: (i, 0, 0, 0))
    return pl.pallas_call(
        functools.partial(_rwkv_scan_body, nbt=nbt, tc=tc),
        grid=(bsz // nbt, seq_len // tc),
        in_specs=[tok] * 6 + [st],
        out_specs=[tok, st],
        out_shape=[jax.ShapeDtypeStruct((bsz, seq_len, w), F32),
                   jax.ShapeDtypeStruct((bsz, A_PAIRS, A_HEAD, LANES), F32)],
        scratch_shapes=[pltpu.VMEM((nbt, A_PAIRS, A_HEAD, LANES), F32),
                        pltpu.VMEM((nbt * A_PAIRS * A_HEAD, min(SUBLANES, tc) * LANES), BF16)],
        compiler_params=_cparams(("parallel", "arbitrary")),
        name="rwkv_scan",
    )(r, d, k, v, kk, b, s0)


def _rwkv_post_body(o_ref, bonus_ref, g_ref, lnw_ref, lnb_ref, out_ref):
    o = o_ref[...]
    ones = _block_ones(F32)
    inv_n = 1.0 / A_HEAD
    mean = _seg_sum(o, ones) * inv_n
    cen = o - mean
    var = _seg_sum(cen * cen, ones) * inv_n
    y = cen * lax.rsqrt(var + A_GN_EPS) * lnw_ref[...] + lnb_ref[...]
    out_ref[...] = ((y + bonus_ref[...]) * g_ref[...]).astype(out_ref.dtype)


def rwkv_post(o, bonus, g, ln_w, ln_b, tr=256):
    t, w = o.shape
    tr = min(tr, t)
    tok = pl.BlockSpec((tr, w), lambda i: (i, 0))
    vec = pl.BlockSpec((1, w), lambda i: (0, 0))
    return pl.pallas_call(
        _rwkv_post_body,
        grid=(pl.cdiv(t, tr),),
        in_specs=[tok, tok, tok, vec, vec],
        out_specs=tok,
        out_shape=jax.ShapeDtypeStruct((t, w), BF16),
        compiler_params=_cparams(("parallel",)),
        name="rwkv_post",
    )(o, bonus, g, ln_w.reshape(1, w), ln_b.reshape(1, w))


def _rope_body(xa_ref, xb_ref, cos_ref, sa_ref, sb_ref, q_ref, k_ref, v_ref, k16_ref, v16_ref, *, shift, nblk):
    cos = cos_ref[...]
    sin_a = sa_ref[...]
    sin_b = sb_ref[...]
    half = B_ROT // 2
    nslab = B_WIDTH // LANES
    lane = lax.broadcasted_iota(jnp.int32, cos.shape, 1)

    def block(i):
        ref, i = (xa_ref, i) if i < nblk else (xb_ref, i - nblk)
        x = ref[0, :, i * LANES:(i + 1) * LANES]
        return pltpu.roll(x, LANES - shift, 1) if shift else x

    nxt = block(0)
    for s in range(3 * nslab):
        if shift:
            cur, nxt = nxt, block(s + 1)
            x = jnp.where(lane < LANES - shift, cur, nxt)
        else:
            x = block(s)
        if s < 2 * nslab:
            x = x * cos + pltpu.roll(x, half, 1) * sin_a + pltpu.roll(x, LANES - half, 1) * sin_b
        if s < nslab:
            q_ref[0, :, s * LANES:(s + 1) * LANES] = (x * (B_QK ** -0.5)).astype(q_ref.dtype)
        elif s < 2 * nslab:
            k_ref[0, :, (s - nslab) * LANES:(s - nslab + 1) * LANES] = x
            k16_ref[0, :, (s - nslab) * LANES:(s - nslab + 1) * LANES] = x.astype(BF16)
        else:
            v_ref[0, :, (s - 2 * nslab) * LANES:(s - 2 * nslab + 1) * LANES] = x
            v16_ref[0, :, (s - 2 * nslab) * LANES:(s - 2 * nslab + 1) * LANES] = x.astype(BF16)


def rope_tables(pos):
    half = B_ROT // 2
    inv = ROPE_THETA ** (-jnp.arange(half, dtype=F32) * 2.0 / B_ROT)
    ang = pos.astype(F32)[:, None] * inv[None, :]
    cos, sin = jnp.cos(ang), jnp.sin(ang)
    n = pos.shape[0]
    one = jnp.ones((n, B_QK - B_ROT), F32)
    zero = jnp.zeros((n, B_QK - B_ROT), F32)
    zh = jnp.zeros((n, half), F32)
    comp = lambda a, b_, rest: jnp.concatenate([a, b_, rest], axis=1)
    cos_p = comp(cos, cos, one)
    sin_a = comp(zh, sin, zero)
    sin_b = comp(-sin, zh, zero)
    dup = lambda x: jnp.concatenate([x, x], axis=1)
    return dup(cos_p), dup(sin_a), dup(sin_b)


def rope_qkv(u, col0, tables, tm=256):
    bsz, seq_len, width = u.shape
    tm = min(tm, seq_len)
    if col0 == 0:
        nblk, first, shift = 3 * B_WIDTH // LANES, 0, 0
    else:
        nblk, first, shift = col0 // LANES, 1, col0 % LANES
        assert 2 * nblk * LANES >= shift + 3 * B_WIDTH + LANES and 3 * nblk * LANES <= width
    win = nblk * LANES
    tab = pl.BlockSpec((tm, LANES), lambda b, j: (j, 0))
    out_spec = pl.BlockSpec((1, tm, B_WIDTH), lambda b, j: (b, j, 0))
    sds = lambda dt: jax.ShapeDtypeStruct((bsz, seq_len, B_WIDTH), dt)
    return pl.pallas_call(
        functools.partial(_rope_body, shift=shift, nblk=nblk),
        grid=(bsz, seq_len // tm),
        in_specs=[pl.BlockSpec((1, tm, win), lambda b, j: (b, j, first)),
                  pl.BlockSpec((1, tm, win), lambda b, j: (b, j, first + 1 if shift else first)),
                  tab, tab, tab],
        out_specs=[out_spec] * 5,
        out_shape=[sds(BF16), sds(F32), sds(F32), sds(BF16), sds(BF16)],
        compiler_params=_cparams(("parallel", "parallel")),
        name="rope_qkv",
    )(u, u, *tables)


def _lam(lqk_ref, lam_init):
    d1 = jnp.sum(lqk_ref[0:1, :] * lqk_ref[1:2, :], axis=-1, keepdims=True)
    d2 = jnp.sum(lqk_ref[2:3, :] * lqk_ref[3:4, :], axis=-1, keepdims=True)
    return jnp.exp(d1) - jnp.exp(d2) + lam_init


def _diff_finish(acc1, l1, acc2, l2, lam, subln, lam_init):
    a = acc1 / l1 - lam * (acc2 / l2)
    y = a * lax.rsqrt(jnp.mean(a * a, axis=-1, keepdims=True) + EPS) * subln
    return y * (1.0 - lam_init)


_NT = (((1,), (1,)), ((), ()))


def _attn_prompt_body(q_ref, k_ref, v_ref, lqk_ref, sub_ref, o_ref, qs_ref, m_ref, l_ref, acc_ref,
                      *, tq, tk, hp, lam_init):
    qi = pl.program_id(2)
    ki = pl.program_id(3)
    nchain = 2 * hp
    head = lambda h: slice(h * B_VDIM, (h + 1) * B_VDIM)

    @pl.when(ki == 0)
    def _():
        lane = lax.broadcasted_iota(jnp.int32, (tq, B_VDIM), 1)
        for h in range(hp):
            qh = q_ref[0, :, head(h)]
            qs_ref[2 * h] = jnp.where(lane < B_QK, qh, jnp.zeros_like(qh))
            qs_ref[2 * h + 1] = jnp.where(lane >= B_QK, qh, jnp.zeros_like(qh))
        m_ref[...] = jnp.full(m_ref.shape, NEG_INF, F32)
        l_ref[...] = jnp.zeros(l_ref.shape, F32)
        acc_ref[...] = jnp.zeros(acc_ref.shape, F32)

    def block(masked):
        ks = [k_ref[0, :, head(h)] for h in range(hp)]
        vs = [v_ref[0, :, head(h)] for h in range(hp)]
        ss = [lax.dot_general(qs_ref[c], ks[c // 2], _NT, preferred_element_type=F32) for c in range(nchain)]
        if masked:
            row = qi * tq + lax.broadcasted_iota(jnp.int32, (tq, tk), 0)
            col = ki * tk + lax.broadcasted_iota(jnp.int32, (tq, tk), 1)
            keep = col <= row
            ss = [jnp.where(keep, s, NEG_INF) for s in ss]
        alphas, ps = [], []
        for c, s in enumerate(ss):
            m_prev = m_ref[c]
            m_new = jnp.maximum(m_prev, jnp.max(s, axis=-1, keepdims=True))
            alpha = jnp.exp(m_prev - m_new)
            p = jnp.exp(s - m_new)
            l_ref[c] = alpha * l_ref[c] + jnp.sum(p, axis=-1, keepdims=True)
            m_ref[c] = m_new
            alphas.append(alpha)
            ps.append(p.astype(BF16))
        for c in range(nchain):
            acc_ref[c] = alphas[c] * acc_ref[c] + jnp.dot(ps[c], vs[c // 2], preferred_element_type=F32)

    visible = ki * tk < (qi + 1) * tq
    full = ki * tk + tk - 1 <= qi * tq

    @pl.when(full)
    def _():
        block(False)

    @pl.when(visible & jnp.logical_not(full))
    def _():
        block(True)

    @pl.when(ki == pl.num_programs(3) - 1)
    def _():
        lam = _lam(lqk_ref, lam_init)
        for h in range(hp):
            y = _diff_finish(acc_ref[2 * h], l_ref[2 * h], acc_ref[2 * h + 1], l_ref[2 * h + 1], lam,
                             sub_ref[...], lam_init)
            o_ref[0, :, head(h)] = y.astype(o_ref.dtype)


def attn_prompt(q, k, v, lqk, subln, lam_init, tq=256, tk=1024, hp=2):
    bsz, seq_len, _ = q.shape
    tq = min(tq, seq_len)
    tk = min(tk, seq_len)
    assert seq_len % tq == 0 and seq_len % tk == 0 and B_HEADS % hp == 0
    nq, nk = seq_len // tq, seq_len // tk
    ngrp = B_HEADS // hp
    kv_blk = lambda qi, ki: jnp.minimum(ki, ((qi + 1) * tq - 1) // tk)
    wide = hp * B_VDIM
    return pl.pallas_call(
        functools.partial(_attn_prompt_body, tq=tq, tk=tk, hp=hp, lam_init=lam_init),
        grid=(bsz, ngrp, nq, nk),
        in_specs=[
            pl.BlockSpec((1, tq, wide), lambda b, g, qi, ki: (b, qi, g)),
            pl.BlockSpec((1, tk, wide), lambda b, g, qi, ki: (b, kv_blk(qi, ki), g)),
            pl.BlockSpec((1, tk, wide), lambda b, g, qi, ki: (b, kv_blk(qi, ki), g)),
            pl.BlockSpec((4, B_QK), lambda b, g, qi, ki: (0, 0)),
            pl.BlockSpec((1, B_VDIM), lambda b, g, qi, ki: (0, 0)),
        ],
        out_specs=pl.BlockSpec((1, tq, wide), lambda b, g, qi, ki: (b, qi, g)),
        out_shape=jax.ShapeDtypeStruct((bsz, seq_len, B_WIDTH), BF16),
        scratch_shapes=[pltpu.VMEM((2 * hp, tq, B_VDIM), BF16), pltpu.VMEM((2 * hp, tq, 1), F32),
                        pltpu.VMEM((2 * hp, tq, 1), F32), pltpu.VMEM((2 * hp, tq, B_VDIM), F32)],
        compiler_params=_cparams(("parallel", "parallel", "parallel", "arbitrary")),
        name="attn_prompt",
    )(q, k, v, lqk, subln.reshape(1, B_VDIM))


def _attn_decode_body(pt_ref, q_ref, kn_ref, vn_ref, ck0_ref, ck1_ref, cv0_ref, cv1_ref, lqk_ref, sub_ref, o_ref,
                      qa_ref, m_ref, l_ref, acc_ref, bias_ref, *, nq, n_steps, lam_init):
    del pt_ref
    p = pl.program_id(1)
    rows = 2 * nq
    nrow = B_HEADS * rows
    bf_round = lambda x: x.astype(BF16).astype(F32)

    @pl.when(p == 0)
    def _():
        q = q_ref[0].astype(F32)
        lane = lax.broadcasted_iota(jnp.int32, (nq, B_VDIM), 1)
        for h in range(B_HEADS):
            qh = q[:, h * B_VDIM:(h + 1) * B_VDIM]
            qa_ref[h * rows:(h + 1) * rows, :] = jnp.concatenate(
                [jnp.where(lane < B_QK, qh, 0.0), jnp.where(lane >= B_QK, qh, 0.0)], axis=0)
        m_ref[...] = jnp.full(m_ref.shape, NEG_INF, F32)
        l_ref[...] = jnp.zeros(l_ref.shape, F32)
        acc_ref[...] = jnp.zeros(acc_ref.shape, F32)
        own = (lax.broadcasted_iota(jnp.int32, bias_ref.shape, 0) // rows
               == lax.broadcasted_iota(jnp.int32, bias_ref.shape, 1) % B_HEADS)
        bias_ref[...] = jnp.where(own, 0.0, NEG_INF)

    @pl.when(p < n_steps)
    def _():
        half = ck0_ref.shape[2] * B_HEADS
        flat = lambda ref: ref[0, 0].reshape(half, B_VDIM).astype(BF16)
        k = jnp.concatenate([flat(ck0_ref), flat(ck1_ref)], axis=0)
        v = jnp.concatenate([flat(cv0_ref), flat(cv1_ref)], axis=0)
        s = lax.dot_general(qa_ref[...].astype(BF16), k, _NT, preferred_element_type=F32) + bias_ref[...]
        m_prev = m_ref[...]
        m_new = jnp.maximum(m_prev, jnp.max(s, axis=-1, keepdims=True))
        alpha = jnp.exp(m_prev - m_new)
        pr = jnp.exp(s - m_new)
        l_ref[...] = alpha * l_ref[...] + jnp.sum(pr, axis=-1, keepdims=True)
        acc_ref[...] = alpha * acc_ref[...] + jnp.dot(pr.astype(BF16), v, preferred_element_type=F32)
        m_ref[...] = m_new

    @pl.when(p == n_steps)
    def _():
        lam = _lam(lqk_ref, lam_init)
        qrow = lax.broadcasted_iota(jnp.int32, (nrow, 1), 0) % nq
        qa = bf_round(qa_ref[...])
        kn = bf_round(kn_ref[0])
        vn = bf_round(vn_ref[0])
        per_head = lambda x, j: jnp.concatenate(
            [jnp.broadcast_to(x[j:j + 1, h * B_VDIM:(h + 1) * B_VDIM], (rows, B_VDIM)) for h in range(B_HEADS)], axis=0)
        ss = [jnp.where(qrow >= j, jnp.sum(qa * per_head(kn, j), axis=-1, keepdims=True), NEG_INF)
              for j in range(nq)]
        m_prev = m_ref[...]
        m_new = m_prev
        for sj in ss:
            m_new = jnp.maximum(m_new, sj)
        alpha = jnp.exp(m_prev - m_new)
        l = alpha * l_ref[...]
        acc = alpha * acc_ref[...]
        for j, sj in enumerate(ss):
            pj = jnp.exp(sj - m_new)
            l = l + pj
            acc = acc + bf_round(pj) * per_head(vn, j)
        n = acc / l
        a = n - lam * pltpu.roll(n, nrow - nq, 0)
        y = a * lax.rsqrt(jnp.mean(a * a, axis=-1, keepdims=True) + EPS) * sub_ref[...] * (1.0 - lam_init)
        for h in range(B_HEADS):
            o_ref[0, :, h * B_VDIM:(h + 1) * B_VDIM] = y[h * rows:h * rows + nq].astype(o_ref.dtype)


def attn_decode(q, k, v, cache_k, cache_v, page_table, layer, lqk, subln, lam_init):
    bsz, nq, _ = q.shape
    n_pages = page_table.shape[1]
    page = cache_k.shape[2]
    nrow = B_HEADS * 2 * nq
    assert n_pages % 2 == 0
    n_steps = n_pages // 2
    cache_spec = lambda odd: pl.BlockSpec(
        (1, 1, page, B_HEADS, B_VDIM),
        lambda b, p, pt: (layer, pt[b * n_pages + 2 * jnp.minimum(p, n_steps - 1) + odd], 0, 0, 0))
    grid_spec = pltpu.PrefetchScalarGridSpec(
        num_scalar_prefetch=1,
        grid=(bsz, n_steps + 1),
        in_specs=[
            pl.BlockSpec((1, nq, B_WIDTH), lambda b, p, pt: (b, 0, 0)),
            pl.BlockSpec((1, nq, B_WIDTH), lambda b, p, pt: (b, 0, 0)),
            pl.BlockSpec((1, nq, B_WIDTH), lambda b, p, pt: (b, 0, 0)),
            cache_spec(0), cache_spec(1), cache_spec(0), cache_spec(1),
            pl.BlockSpec((4, B_QK), lambda b, p, pt: (0, 0)),
            pl.BlockSpec((1, B_VDIM), lambda b, p, pt: (0, 0)),
        ],
        out_specs=pl.BlockSpec((1, nq, B_WIDTH), lambda b, p, pt: (b, 0, 0)),
        scratch_shapes=[
            pltpu.VMEM((nrow, B_VDIM), F32),
            pltpu.VMEM((nrow, 1), F32),
            pltpu.VMEM((nrow, 1), F32),
            pltpu.VMEM((nrow, B_VDIM), F32),
            pltpu.VMEM((nrow, 2 * page * B_HEADS), F32),
        ],
    )
    return pl.pallas_call(
        functools.partial(_attn_decode_body, nq=nq, n_steps=n_steps, lam_init=lam_init),
        grid_spec=grid_spec,
        out_shape=jax.ShapeDtypeStruct((bsz, nq, B_WIDTH), F32),
        compiler_params=_cparams(("parallel", "arbitrary")),
        name="attn_decode",
    )(page_table.reshape(-1), q, k, v, cache_k, cache_k, cache_v, cache_v, lqk, subln.reshape(1, B_VDIM)).astype(BF16)


_TN = (((0,), (0,)), ((), ()))


def _hgrn_body(qa_ref, qb_ref, fa_ref, fb_ref, ia_ref, ib_ref, ga_ref, gb_ref, lb_ref, cn_ref, s0_ref,
               o_ref, sfin_ref, s_ref, *, tc, n_valid, shift):
    c = pl.program_id(2)
    lane_in = lax.broadcasted_iota(jnp.int32, (tc, LANES), 1)

    def slab(a_ref, b_ref):
        if not shift:
            return a_ref[0]
        return jnp.where(lane_in < LANES - shift, pltpu.roll(a_ref[0], LANES - shift, 1),
                         pltpu.roll(b_ref[0], LANES - shift, 1))

    @pl.when(c == 0)
    def _():
        s_ref[...] = s0_ref[0, 0]

    nrow = -(-tc // C_CHUNK) * C_CHUNK
    pad = lambda x: x if nrow == tc else jnp.concatenate([x, jnp.zeros((nrow - tc, x.shape[1]), F32)], axis=0)
    q = _silu(pad(slab(qa_ref, qb_ref)))
    lb = lb_ref[...]
    fs = lb + (1.0 - lb) * _sigmoid(pad(slab(fa_ref, fb_ref)))
    kin = 1.0 - fs
    gl = jnp.log(fs)
    iv = pad(slab(ia_ref, ib_ref))
    row = lax.broadcasted_iota(jnp.int32, (nrow, C_DK), 0)
    if n_valid < nrow:
        kin = jnp.where(row < n_valid, kin, 0.0)
        gl = jnp.where(row < n_valid, gl, 0.0)
    b = gl
    sh = 1
    while sh < C_CHUNK:
        b = b + jnp.where(row % C_CHUNK >= sh, pltpu.roll(b, sh, 0), 0.0)
        sh *= 2
    trow = lax.broadcasted_iota(jnp.int32, (C_CHUNK, C_DK), 0)
    eye = (lax.broadcasted_iota(jnp.int32, (C_DK, C_DK), 0)
           == lax.broadcasted_iota(jnp.int32, (C_DK, C_DK), 1)).astype(F32)
    outs = []
    for ch in range(nrow // C_CHUNK):
        rs = slice(ch * C_CHUNK, (ch + 1) * C_CHUNK)
        bc, qc, kc, ic = b[rs], q[rs], kin[rs], iv[rs]
        s = s_ref[...]
        o = jnp.dot((qc * jnp.exp(bc)).astype(BF16), s.astype(BF16), preferred_element_type=F32)
        for sidx in range(C_CHUNK):
            dec = jnp.exp(jnp.where(trow >= sidx, bc - bc[sidx:sidx + 1, :], NEG_INF))
            att = jnp.sum(qc * kc[sidx:sidx + 1, :] * dec, axis=-1, keepdims=True)
            o = o + att * ic[sidx:sidx + 1, :]
        bl = bc[C_CHUNK - 1:C_CHUNK, :]
        kd = kc * jnp.exp(bl - bc)
        inc = lax.dot_general(kd.astype(BF16), ic.astype(BF16), _TN, preferred_element_type=F32)
        ecol = jnp.sum(eye * jnp.exp(bl), axis=1, keepdims=True)
        s_ref[...] = s * ecol + inc
        outs.append(o)
    o = jnp.concatenate(outs, axis=0)[0:tc]
    y = o * lax.rsqrt(jnp.mean(o * o, axis=-1, keepdims=True) + EPS) * cn_ref[...]
    o_ref[0] = (y * _silu(slab(ga_ref, gb_ref))).astype(o_ref.dtype)

    @pl.when(c == pl.num_programs(2) - 1)
    def _():
        sfin_ref[0, 0] = s_ref[...]


def hgrn(u, col0, s0, lb, c_norm, tc=256):
    bsz, seq_len, _ = u.shape
    tc = min(tc, seq_len)
    assert seq_len % tc == 0
    blk0, shift = col0 // LANES, col0 % LANES
    tok = lambda off: pl.BlockSpec((1, tc, LANES), lambda b, h, c: (b, c, blk0 + off + h))
    pair = lambda grp: [tok(grp * C_HEADS), tok(grp * C_HEADS + (1 if shift else 0))]
    st = pl.BlockSpec((1, 1, C_DK, C_DV), lambda b, h, c: (b, h, 0, 0))
    return pl.pallas_call(
        functools.partial(_hgrn_body, tc=tc, n_valid=tc, shift=shift),
        grid=(bsz, C_HEADS, seq_len // tc),
        in_specs=pair(0) + pair(1) + pair(2) + pair(3) + [
            pl.BlockSpec((1, LANES), lambda b, h, c: (0, h)),
            pl.BlockSpec((1, C_DV), lambda b, h, c: (0, 0)),
            st],
        out_specs=[pl.BlockSpec((1, tc, LANES), lambda b, h, c: (b, c, h)), st],
        out_shape=[jax.ShapeDtypeStruct((bsz, seq_len, C_WIDTH), BF16),
                   jax.ShapeDtypeStruct((bsz, C_HEADS, C_DK, C_DV), F32)],
        scratch_shapes=[pltpu.VMEM((C_DK, C_DV), F32)],
        compiler_params=_cparams(("parallel", "parallel", "arbitrary")),
        name="hgrn",
    )(*([u] * 8), lb.reshape(1, C_FK), c_norm.reshape(1, C_DV), s0)


def _pair_state(s):
    bsz = s.shape[0]
    return s.reshape(bsz, A_PAIRS, 2, A_HEAD, A_HEAD).transpose(0, 1, 3, 2, 4).reshape(bsz, A_PAIRS, A_HEAD, LANES)


def _unpair_state(s):
    bsz = s.shape[0]
    return s.reshape(bsz, A_PAIRS, A_HEAD, 2, A_HEAD).transpose(0, 1, 3, 2, 4).reshape(bsz, A_HEADS, A_HEAD, A_HEAD)


def _layer_params(l, P):
    padc = A_COLS_PAD - A_COLS
    w2 = jnp.zeros((LANES, 2 * A_WIDTH), F32)
    w2 = w2.at[0:A_DECAY_R, 0:A_WIDTH].set(P["a_w_up"][l])
    w2 = w2.at[A_DECAY_R:A_DECAY_R + A_ICLR_R, A_WIDTH:].set(P["a_a_up"][l])
    return dict(
        mu=jnp.pad(P["a_mu"][l], (0, padc)),
        w2=w2,
        gup=jnp.pad(P["a_g_up"][l], ((0, 2 * LANES - A_GATE_R), (0, 0))),
        lqk=jnp.stack([P["b_lq1"][l], P["b_lk1"][l], P["b_lq2"][l], P["b_lk2"][l]]),
        r_k=P["a_r_k"][l].reshape(A_WIDTH),
    )


def _run_trunk(x, pos, caches, wkv0, shift0, hgrn0, conv0, lbs, P, LP):
    bsz, seq_len, d = x.shape
    t = bsz * seq_len
    h = x.reshape(t, d)
    tables = rope_tables(pos)
    padc = A_COLS_PAD - A_COLS
    ks, vs, wkvs, shifts, hgrns, convs = [], [], [], [], [], []
    for l in range(DEPTH):
        lp = LP[l]
        hn = rmsnorm_rows(h, P["norm_mix"][l], BF16)
        u = matmul(hn, P["w_in"], layer=l, pad_n=True)
        u = u.reshape(bsz, seq_len, u.shape[-1])
        r, dcy, kp, v, kk, bb, g, bonus = rwkv_pre(
            u, jnp.pad(shift0[l], ((0, 0), (0, padc))), lp["mu"], lp["w2"], lp["gup"],
            P["a_w0"][l], P["a_a0"][l], P["a_k_k"][l], P["a_k_a"][l], lp["r_k"])
        o_a, s_a = rwkv_scan(r, dcy, kp, v, kk, bb, _pair_state(wkv0[l]))
        flat = lambda z: z.reshape(t, A_WIDTH)
        oa = rwkv_post(flat(o_a), flat(bonus), flat(g), P["a_ln_w"][l], P["a_ln_b"][l])
        lam_init = 0.8 - 0.6 * math.exp(-0.3 * l)
        qb, kb, vb, kb16, vb16 = rope_qkv(u, A_COLS, tables)
        if caches is None:
            ob = attn_prompt(qb, kb16, vb16, lp["lqk"], P["b_subln"][l], lam_init)
        else:
            ob = attn_decode(qb, kb, vb, caches[0], caches[1], caches[2], l, lp["lqk"], P["b_subln"][l], lam_init)
        oc, s_c = hgrn(u, A_COLS + 3 * B_WIDTH, hgrn0[l], lbs[l], P["c_norm"][l])
        mix = jnp.concatenate([oa, ob.reshape(t, B_WIDTH), oc.reshape(t, C_WIDTH)], axis=-1)
        h = matmul(mix, P["w_out"], layer=l, res=h)
        hf = rmsnorm_rows(h, P["norm_ffn"][l], BF16)
        mid, buf = ffn_up(hf, P["f_gate"], P["f_up"], P["f_conv"], P["f_conv_b"], conv0[l], seq_len, layer=l)
        h = matmul_kloop(mid, P["f_down"], l, h)
        ks.append(kb.reshape(bsz, seq_len, B_HEADS, B_VDIM))
        vs.append(vb.reshape(bsz, seq_len, B_HEADS, B_VDIM))
        wkvs.append(_unpair_state(s_a))
        shifts.append(u[:, -1, :A_COLS])
        hgrns.append(s_c)
        convs.append(buf)
    y = rmsnorm_rows(h, P["norm_final"], F32).reshape(bsz, seq_len, d)
    st = jnp.stack
    return y, st(ks), st(vs), st(wkvs), st(shifts), st(hgrns), st(convs)


def kernel(x_prompt, x_sample, cache_k, cache_v, page_table, state_wkv, state_shift, state_hgrn, state_conv,
           norm_mix, w_in, a_mu, a_w0, a_w_up, a_a0, a_a_up, a_g_up, a_k_k, a_k_a, a_r_k, a_ln_w, a_ln_b,
           b_lq1, b_lk1, b_lq2, b_lk2, b_subln, c_lb_logits, c_norm, w_out, norm_ffn, f_gate, f_up,
           f_conv, f_conv_b, f_down, norm_final):
    P = dict(norm_mix=norm_mix, w_in=w_in, a_mu=a_mu, a_w0=a_w0, a_w_up=a_w_up, a_a0=a_a0, a_a_up=a_a_up,
             a_g_up=a_g_up, a_k_k=a_k_k, a_k_a=a_k_a, a_r_k=a_r_k, a_ln_w=a_ln_w, a_ln_b=a_ln_b,
             b_lq1=b_lq1, b_lk1=b_lk1, b_lq2=b_lq2, b_lk2=b_lk2, b_subln=b_subln, c_norm=c_norm,
             w_out=w_out, norm_ffn=norm_ffn, f_gate=f_gate, f_up=f_up, f_conv=f_conv, f_conv_b=f_conv_b,
             f_down=f_down, norm_final=norm_final)
    LP = [_layer_params(l, P) for l in range(DEPTH)]
    lbs = lower_bounds(c_lb_logits)
    bp, lp_ = x_prompt.shape[:2]
    past_len = page_table.shape[1] * cache_k.shape[2]
    pos_p = jnp.arange(lp_)
    pos_s = past_len + jnp.arange(x_sample.shape[1])
    zeros = lambda *s: jnp.zeros(s, F32)
    out_p = _run_trunk(
        x_prompt, pos_p, None,
        zeros(DEPTH, bp, A_HEADS, A_HEAD, A_HEAD), zeros(DEPTH, bp, A_COLS),
        zeros(DEPTH, bp, C_HEADS, C_DK, C_DV), zeros(DEPTH, bp, CONV_W - 1, D_FF), lbs, P, LP)
    caches = (cache_k, cache_v, page_table)
    out_s = _run_trunk(x_sample, pos_s, caches, state_wkv, state_shift, state_hgrn, state_conv, lbs, P, LP)
    y_p, k_p, v_p, wkv_p, sh_p, hg_p, cv_p = out_p
    y_s, k_s, v_s, wkv_s, sh_s, hg_s, cv_s = out_s
    return (y_p, y_s, k_p, v_p, k_s, v_s, wkv_p, wkv_s, sh_p, sh_s, hg_p, hg_s, cv_p, cv_s)
```

```python
import functools
import math

import jax
import jax.numpy as jnp
from jax import lax
from jax.experimental import pallas as pl
from jax.experimental.pallas import tpu as pltpu

F32 = jnp.float32
BF16 = jnp.bfloat16

V7X_VMEM_BYTES = 64 * 1024 * 1024
VMEM_LIMIT = V7X_VMEM_BYTES - 8 * 1024 * 1024
LANES = 128
SUBLANES = 8

D_MODEL = 4096
DEPTH = 4
A_WIDTH = D_MODEL // 4
A_HEAD = 64
A_HEADS = A_WIDTH // A_HEAD
A_PAIRS = A_HEADS // 2
A_DECAY_R = 64
A_ICLR_R = 64
A_GATE_R = 160
A_COLS = 3 * A_WIDTH + A_DECAY_R + A_ICLR_R + A_GATE_R
A_COLS_PAD = 3 * A_WIDTH + 3 * LANES
A_GN_EPS = 64e-5
B_WIDTH = D_MODEL // 2
B_HEADS = 16
B_VDIM = B_WIDTH // B_HEADS
B_QK = B_VDIM // 2
B_ROT = B_QK // 4
ROPE_THETA = 500000.0
C_WIDTH = D_MODEL - A_WIDTH - B_WIDTH
C_HEADS = 8
C_DK = 128
C_DV = C_WIDTH // C_HEADS
C_FK = C_HEADS * C_DK
C_CHUNK = 16
D_FF = 11008
CONV_W = 3
EPS = 1e-5
NEG_INF = float("-inf")


def _cparams(sem):
    return pltpu.CompilerParams(dimension_semantics=sem, vmem_limit_bytes=VMEM_LIMIT)


def _sigmoid(x):
    return 1.0 / (1.0 + jnp.exp(-x))


def _silu(x):
    return x * _sigmoid(x)


def _block_ones(dtype):
    r = lax.broadcasted_iota(jnp.int32, (LANES, LANES), 0) // A_HEAD
    c = lax.broadcasted_iota(jnp.int32, (LANES, LANES), 1) // A_HEAD
    return (r == c).astype(dtype)


def _rmsnorm_body(x_ref, w_ref, o_ref):
    x = x_ref[...]
    y = x * lax.rsqrt(jnp.mean(x * x, axis=-1, keepdims=True) + EPS)
    o_ref[...] = (y * w_ref[...]).astype(o_ref.dtype)


def rmsnorm_rows(x, w, out_dtype, tr=256):
    t, d = x.shape
    tr = min(tr, t)
    return pl.pallas_call(
        _rmsnorm_body,
        grid=(pl.cdiv(t, tr),),
        in_specs=[pl.BlockSpec((tr, d), lambda i: (i, 0)), pl.BlockSpec((1, d), lambda i: (0, 0))],
        out_specs=pl.BlockSpec((tr, d), lambda i: (i, 0)),
        out_shape=jax.ShapeDtypeStruct((t, d), out_dtype),
        compiler_params=_cparams(("parallel",)),
        name="rmsnorm_rows",
    )(x, w.reshape(1, d))


def _mm_body(*refs, nk, tk, k_total, n_total, has_res):
    if has_res:
        a_ref, w_ref, r_ref, o_ref = refs
    else:
        a_ref, w_ref, o_ref = refs
        r_ref = None
    a = a_ref[...]
    w = w_ref[...].astype(BF16)
    tn = w.shape[1]
    if n_total % tn:
        valid = n_total - pl.program_id(1) * tn
        w = jnp.where(lax.broadcasted_iota(jnp.int32, (1, tn), 1) < valid, w, jnp.zeros_like(w))
    if nk == 1:
        acc = jnp.dot(a, w, preferred_element_type=F32)
        if has_res:
            acc = acc + r_ref[...]
        o_ref[...] = acc.astype(o_ref.dtype)
        return
    k = pl.program_id(2)
    if k_total % tk:
        valid = k_total - k * tk
        a = jnp.where(lax.broadcasted_iota(jnp.int32, a.shape, 1) < valid, a, jnp.zeros_like(a))
        w = jnp.where(lax.broadcasted_iota(jnp.int32, w.shape, 0) < valid, w, jnp.zeros_like(w))
    part = jnp.dot(a, w, preferred_element_type=F32)

    @pl.when(k == 0)
    def _():
        o_ref[...] = (part + r_ref[...]) if has_res else part

    @pl.when(k > 0)
    def _():
        o_ref[...] += part


def matmul(a, w, layer=None, res=None, out_dtype=F32, tm=2048, tn=256, tk=None, pad_n=False):
    m, kdim = a.shape
    n = w.shape[-1]
    tm = min(tm, m)
    tk = kdim if tk is None else tk
    nk = pl.cdiv(kdim, tk)
    if nk > 1:
        assert out_dtype == F32
    grid = (pl.cdiv(m, tm), pl.cdiv(n, tn), nk)
    n_out = grid[1] * tn if pad_n else n
    once = dict(pipeline_mode=pl.Buffered(1))
    a_kw = once if (nk == 1 and grid[1] > 1) else {}
    if layer is None:
        w_spec = pl.BlockSpec((tk, tn), lambda i, j, k: (k, j))
    else:
        w_spec = pl.BlockSpec((None, tk, tn), lambda i, j, k: (layer, k, j))
    in_specs = [pl.BlockSpec((tm, tk), lambda i, j, k: (i, k), **a_kw), w_spec]
    args = [a, w]
    if res is not None:
        in_specs.append(pl.BlockSpec((tm, tn), lambda i, j, k: (i, j), **(once if nk > 1 else {})))
        args.append(res)
    return pl.pallas_call(
        functools.partial(_mm_body, nk=nk, tk=tk, k_total=kdim, n_total=n, has_res=res is not None),
        grid=grid,
        in_specs=in_specs,
        out_specs=pl.BlockSpec((tm, tn), lambda i, j, k: (i, j)),
        out_shape=jax.ShapeDtypeStruct((m, n_out), out_dtype),
        compiler_params=_cparams(("parallel", "parallel", "arbitrary")),
        name="matmul",
    )(*args)


def _mm_kloop_body(a_ref, w_ref, r_ref, o_ref, *, tk):
    k_total = a_ref.shape[1]
    acc = r_ref[...]
    for k0 in range(0, k_total, tk):
        kw = min(tk, k_total - k0)
        acc = acc + jnp.dot(a_ref[:, k0:k0 + kw], w_ref[k0:k0 + kw, :].astype(BF16), preferred_element_type=F32)
    o_ref[...] = acc


def matmul_kloop(a, w, layer, res, tm=1024, tn=256, tk=1024):
    m, kdim = a.shape
    n = w.shape[-1]
    tm = min(tm, m)
    assert m % tm == 0 and n % tn == 0
    return pl.pallas_call(
        functools.partial(_mm_kloop_body, tk=tk),
        grid=(m // tm, n // tn),
        in_specs=[pl.BlockSpec((tm, kdim), lambda i, j: (i, 0), pipeline_mode=pl.Buffered(1)),
                  pl.BlockSpec((None, kdim, tn), lambda i, j: (layer, 0, j)),
                  pl.BlockSpec((tm, tn), lambda i, j: (i, j))],
        out_specs=pl.BlockSpec((tm, tn), lambda i, j: (i, j)),
        out_shape=jax.ShapeDtypeStruct((m, n), F32),
        compiler_params=_cparams(("parallel", "parallel")),
        name="matmul_kloop",
    )(a, w, res)


def _ffn_up_body(a_ref, wg_ref, wu_ref, cw_ref, cb_ref, buf_ref, o_ref, nbuf_ref, ext_ref, *, nb, seq_len):
    tm = nb * seq_len
    a = a_ref[...]
    gate = jnp.dot(a, wg_ref[...].astype(BF16), preferred_element_type=F32)
    up = jnp.dot(a, wu_ref[...].astype(BF16), preferred_element_type=F32)
    ext_ref[0:SUBLANES, :] = jnp.zeros((SUBLANES, gate.shape[1]), F32)
    ext_ref[SUBLANES:SUBLANES + tm, :] = gate
    g1 = ext_ref[SUBLANES - 1:SUBLANES - 1 + tm, :]
    g2 = ext_ref[SUBLANES - 2:SUBLANES - 2 + tm, :]
    row = lax.broadcasted_iota(jnp.int32, gate.shape, 0)
    for s in range(nb):
        b0 = buf_ref[s, 0:1, :]
        b1 = buf_ref[s, 1:2, :]
        g1 = jnp.where(row == s * seq_len, b1, g1)
        g2 = jnp.where(row == s * seq_len, b0, g2)
        g2 = jnp.where(row == s * seq_len + 1, b1, g2)
        last = SUBLANES + (s + 1) * seq_len
        nbuf_ref[s, :, :] = ext_ref[last - 2:last, :]
    conv = cb_ref[...] + g2 * cw_ref[0:1, :] + g1 * cw_ref[1:2, :] + gate * cw_ref[2:3, :]
    o_ref[...] = (_silu(conv) * up).astype(o_ref.dtype)


def ffn_up(hn, w_gate, w_up, w_conv, b_conv, buf, seq_len, layer=None, tn=256):
    t, d = hn.shape
    n = w_gate.shape[-1]
    nseq = buf.shape[0]
    nb = 1 if seq_len >= 256 else nseq
    tm = nb * seq_len
    assert seq_len >= 2 and t == nseq * seq_len and n % tn == 0
    grid = (t // tm, n // tn)
    if layer is None:
        wspec = lambda rows: pl.BlockSpec((rows, tn), lambda i, j: (0, j))
        b_conv = b_conv.reshape(1, n)
    else:
        wspec = lambda rows: pl.BlockSpec((None, rows, tn), lambda i, j: (layer, 0, j))
        b_conv = b_conv.reshape(-1, 1, n)
    return pl.pallas_call(
        functools.partial(_ffn_up_body, nb=nb, seq_len=seq_len),
        grid=grid,
        in_specs=[
            pl.BlockSpec((tm, d), lambda i, j: (i, 0), pipeline_mode=pl.Buffered(1)),
            wspec(d), wspec(d), wspec(CONV_W), wspec(1),
            pl.BlockSpec((nb, CONV_W - 1, tn), lambda i, j: (i, 0, j)),
        ],
        out_specs=[
            pl.BlockSpec((tm, tn), lambda i, j: (i, j)),
            pl.BlockSpec((nb, CONV_W - 1, tn), lambda i, j: (i, 0, j)),
        ],
        out_shape=[
            jax.ShapeDtypeStruct((t, n), BF16),
            jax.ShapeDtypeStruct((nseq, CONV_W - 1, n), F32),
        ],
        scratch_shapes=[pltpu.VMEM((SUBLANES + tm, tn), F32)],
        compiler_params=_cparams(("parallel", "parallel")),
        name="ffn_up",
    )(hn, w_gate, w_up, w_conv, b_conv, buf)


def _lower_bounds_body(x_ref, o_ref):
    x = x_ref[...]
    e = jnp.exp(x - jnp.max(x, axis=0, keepdims=True))
    p = e / jnp.sum(e, axis=0, keepdims=True)
    run = jnp.zeros_like(p[0:1, :])
    for l in range(x.shape[0]):
        run = run + p[l:l + 1, :]
        o_ref[l:l + 1, :] = run - p[0:1, :]


def lower_bounds(logits):
    return pl.pallas_call(
        _lower_bounds_body,
        out_shape=jax.ShapeDtypeStruct(logits.shape, F32),
        name="lower_bounds",
    )(logits)


def _seg_sum(x, ones):
    parts = []
    for s in range(x.shape[1] // LANES):
        parts.append(jnp.dot(x[:, s * LANES:(s + 1) * LANES], ones,
                             preferred_element_type=F32, precision=lax.Precision.HIGHEST))
    return jnp.concatenate(parts, axis=1)


def _rwkv_pre_body(u_ref, prev_ref, mu_ref, w2_ref, gup_ref, w0_ref, a0_ref, kk_ref, ka_ref, rk_ref,
                   r_o, d_o, k_o, v_o, kk_o, b_o, g_o, bonus_o, ext_ref, *, tm):
    j = pl.program_id(1)
    u = u_ref[0]

    @pl.when(j == 0)
    def _():
        ext_ref[SUBLANES - 1:SUBLANES, :] = prev_ref[0]

    ext_ref[SUBLANES:SUBLANES + tm, :] = u
    shifted = ext_ref[SUBLANES - 1:SUBLANES - 1 + tm, :]
    ext_ref[SUBLANES - 1:SUBLANES, :] = u[tm - 1:tm, :]
    xm = u + (shifted - u) * mu_ref[...]
    w = A_WIDTH
    r = xm[:, 0:w]
    k = xm[:, w:2 * w]
    v = xm[:, 2 * w:3 * w]
    lr = xm[:, 3 * w:3 * w + LANES]
    gd = xm[:, 3 * w + LANES:3 * w + 3 * LANES]
    lane = lax.broadcasted_iota(jnp.int32, lr.shape, 1)
    lr = jnp.where(lane < A_DECAY_R, jnp.tanh(lr), lr)
    wa = jnp.dot(lr.astype(BF16), w2_ref[...].astype(BF16), preferred_element_type=F32)
    wraw = -(w0_ref[...] + wa[:, 0:w])
    softplus = jnp.maximum(wraw, 0.0) + jnp.log(1.0 + jnp.exp(-jnp.abs(wraw)))
    decay = jnp.exp(-jnp.exp(-softplus - 0.5))
    a = _sigmoid(a0_ref[...] + wa[:, w:2 * w])
    g = jnp.dot(_sigmoid(gd).astype(BF16), gup_ref[...].astype(BF16), preferred_element_type=F32)
    ones = _block_ones(F32)
    kk = k * kk_ref[...]
    norm = jnp.sqrt(_seg_sum(kk * kk, ones))
    kk = kk / jnp.maximum(norm, 1e-12)
    kp = k * (1.0 + (a - 1.0) * ka_ref[...])
    bonus = _seg_sum(r * kp * rk_ref[...], ones) * v
    r_o[0] = r
    d_o[0] = decay
    k_o[0] = kp
    v_o[0] = v
    kk_o[0] = kk
    b_o[0] = kk * a
    g_o[0] = g
    bonus_o[0] = bonus


def rwkv_pre(ua, prev, mu, w2, gup, w0, a0, k_k, k_a, r_k, tm=256):
    bsz, seq_len, width = ua.shape
    cols = A_COLS_PAD
    tm = min(tm, seq_len)
    assert seq_len % tm == 0 and width % cols == 0
    w = A_WIDTH
    row = lambda x: x.reshape(1, -1)
    vec = lambda n: pl.BlockSpec((1, n), lambda b, j: (0, 0))
    out_spec = pl.BlockSpec((1, tm, w), lambda b, j: (b, j, 0))
    out_sds = jax.ShapeDtypeStruct((bsz, seq_len, w), F32)
    return pl.pallas_call(
        functools.partial(_rwkv_pre_body, tm=tm),
        grid=(bsz, seq_len // tm),
        in_specs=[
            pl.BlockSpec((1, tm, cols), lambda b, j: (b, j, 0)),
            pl.BlockSpec((1, 1, cols), lambda b, j: (b, 0, 0)),
            vec(cols),
            pl.BlockSpec((LANES, 2 * w), lambda b, j: (0, 0)),
            pl.BlockSpec((2 * LANES, w), lambda b, j: (0, 0)),
            vec(w), vec(w), vec(w), vec(w), vec(w),
        ],
        out_specs=[out_spec] * 8,
        out_shape=[out_sds] * 8,
        scratch_shapes=[pltpu.VMEM((SUBLANES + tm, cols), F32)],
        compiler_params=_cparams(("parallel", "arbitrary")),
        name="rwkv_pre",
    )(ua, prev.reshape(bsz, 1, cols), row(mu), w2, gup, row(w0), row(a0), row(k_k), row(k_a), row(r_k))


def _rwkv_scan_body(r_ref, d_ref, k_ref, v_ref, kk_ref, b_ref, s0_ref, o_ref, sfin_ref, s_ref, p2_ref, *, nbt, tc):
    c = pl.program_id(1)

    @pl.when(c == 0)
    def _():
        s_ref[...] = s0_ref[...]

    sub = min(SUBLANES, tc)
    npair = nbt * A_PAIRS
    ones = _block_ones(BF16)
    ones2 = jnp.concatenate([ones, ones], axis=0)
    ones4 = (lax.broadcasted_iota(jnp.int32, (2 * LANES, 2 * LANES), 0) // A_HEAD
             == lax.broadcasted_iota(jnp.int32, (2 * LANES, 2 * LANES), 1) // A_HEAD).astype(BF16)
    eye2 =(lax.broadcasted_iota(jnp.int32, (A_HEAD, LANES), 0)
            == lax.broadcasted_iota(jnp.int32, (A_HEAD, LANES), 1) % A_HEAD)
    eye2b = jnp.where(eye2, 1.0, 0.0).astype(BF16)
    wr = lax.broadcasted_iota(jnp.int32, (2 * sub, sub * LANES), 0)
    wc = lax.broadcasted_iota(jnp.int32, (2 * sub, sub * LANES), 1)
    wsel = jnp.where((wc // LANES == wr % sub) & ((wc % LANES) // A_HEAD == wr // sub), 1.0, 0.0).astype(BF16)
    bc = lambda x: jnp.broadcast_to(x, (A_HEAD, LANES))
    prs = [divmod(q, A_PAIRS) for q in range(npair)]

    nhalf = 2 if npair % 4 == 0 else 1
    per = npair // nhalf

    def group(gi, carry):
        rows = pl.ds(pl.multiple_of(gi * sub, sub), sub)
        tile = lambda ref, q: ref[prs[q][0], rows, pl.ds(prs[q][1] * LANES, LANES)]
        row = lambda x8, j: x8[j:j + 1, :]

        def issue(half, j):
            p1s, p3s = [], []
            for q in range(half * per, (half + 1) * per):
                nb, p = prs[q]
                v8 = tile(v_ref, q)
                v_hi8 = v8.astype(BF16)
                v_lo8 = (v8 - v_hi8.astype(F32)).astype(BF16)
                p1s.append((s_ref[nb, p] * bc(row(tile(kk_ref, q), j))).astype(BF16))
                p3s.append(jnp.concatenate([eye2b * bc(row(v_hi8, j)), eye2b * bc(row(v_lo8, j))], axis=1))
            if per % 2 == 0:
                lhs = jnp.concatenate([jnp.concatenate(p1s[i:i + 2], axis=1) for i in range(0, per, 2)], axis=0)
                sa = jnp.dot(lhs, ones4, preferred_element_type=F32)
                sas = [sa[(i // 2) * A_HEAD:(i // 2 + 1) * A_HEAD, (i % 2) * LANES:(i % 2 + 1) * LANES]
                       for i in range(per)]
            else:
                sa = jnp.dot(jnp.concatenate(p1s, axis=0), ones, preferred_element_type=F32)
                sas = [sa[i * A_HEAD:(i + 1) * A_HEAD] for i in range(per)]
            vc = jnp.dot(jnp.concatenate(p3s, axis=0), ones2, preferred_element_type=F32)
            return sas, [vc[i * A_HEAD:(i + 1) * A_HEAD] for i in range(per)]

        def update(half, j, sas, vcs):
            for i, q in enumerate(range(half * per, (half + 1) * per)):
                nb, p = prs[q]
                s = (s_ref[nb, p] * bc(row(tile(d_ref, q), j)) - sas[i] * bc(row(tile(b_ref, q), j))
                     + vcs[i] * bc(row(tile(k_ref, q), j)))
                s_ref[nb, p] = s
                p2_ref[q * A_HEAD:(q + 1) * A_HEAD, j * LANES:(j + 1) * LANES] = (
                    s * bc(row(tile(r_ref, q), j))).astype(BF16)

        pend = [issue(0, 0)] + [None] * (nhalf - 1)
        for j in range(sub):
            for half in range(1, nhalf):
                pend[half] = issue(half, j)
            update(0, j, *pend[0])
            if j + 1 < sub:
                pend[0] = issue(0, j + 1)
            for half in range(1, nhalf):
                update(half, j, *pend[half])
        for q, (nb, p) in enumerate(prs):
            qs = slice(q * A_HEAD, (q + 1) * A_HEAD)
            ot = lax.dot_general(wsel, p2_ref[qs, :], _NT, preferred_element_type=F32)
            o_ref[nb, rows, pl.ds(p * LANES, LANES)] = jnp.concatenate([ot[0:sub], ot[sub:2 * sub]], axis=1)
        return carry

    lax.fori_loop(0, tc // sub, group, 0)

    @pl.when(c == pl.num_programs(1) - 1)
    def _():
        sfin_ref[...] = s_ref[...]


def rwkv_scan(r, d, k, v, kk, b, s0, tc=64, nbt=4):
    bsz, seq_len, w = r.shape
    tc = min(tc, seq_len)
    nbt = min(nbt, bsz)
    assert seq_len % tc == 0 and bsz % nbt == 0 and w == A_WIDTH
    tok = pl.BlockSpec((nbt, tc, w), lambda i, c: (i, c, 0))
    st = pl.BlockSpec((nbt, A_PAIRS, A_HEAD, LANES), lambda i, c: (i, 0, 0, 0))
    return pl.pallas_call(
        functools.partial(_rwkv_scan_body, nbt=nbt, tc=tc),
        grid=(bsz // nbt, seq_len // tc),
        in_specs=[tok] * 6 + [st],
        out_specs=[tok, st],
        out_shape=[jax.ShapeDtypeStruct((bsz, seq_len, w), F32),
                   jax.ShapeDtypeStruct((bsz, A_PAIRS, A_HEAD, LANES), F32)],
        scratch_shapes=[pltpu.VMEM((nbt, A_PAIRS, A_HEAD, LANES), F32),
                        pltpu.VMEM((nbt * A_PAIRS * A_HEAD, min(SUBLANES, tc) * LANES), BF16)],
        compiler_params=_cparams(("parallel", "arbitrary")),
        name="rwkv_scan",
    )(r, d, k, v, kk, b, s0)


def _rwkv_post_body(o_ref, bonus_ref, g_ref, lnw_ref, lnb_ref, out_ref):
    o = o_ref[...]
    ones = _block_ones(F32)
    inv_n = 1.0 / A_HEAD
    mean = _seg_sum(o, ones) * inv_n
    cen = o - mean
    var = _seg_sum(cen * cen, ones) * inv_n
    y = cen * lax.rsqrt(var + A_GN_EPS) * lnw_ref[...] + lnb_ref[...]
    out_ref[...] = ((y + bonus_ref[...]) * g_ref[...]).astype(out_ref.dtype)


def rwkv_post(o, bonus, g, ln_w, ln_b, tr=256):
    t, w = o.shape
    tr = min(tr, t)
    tok = pl.BlockSpec((tr, w), lambda i: (i, 0))
    vec = pl.BlockSpec((1, w), lambda i: (0, 0))
    return pl.pallas_call(
        _rwkv_post_body,
        grid=(pl.cdiv(t, tr),),
        in_specs=[tok, tok, tok, vec, vec],
        out_specs=tok,
        out_shape=jax.ShapeDtypeStruct((t, w), BF16),
        compiler_params=_cparams(("parallel",)),
        name="rwkv_post",
    )(o, bonus, g, ln_w.reshape(1, w), ln_b.reshape(1, w))


def _rope_body(xa_ref, xb_ref, cos_ref, sa_ref, sb_ref, q_ref, k_ref, v_ref, k16_ref, v16_ref, *, shift, nblk):
    cos = cos_ref[...]
    sin_a = sa_ref[...]
    sin_b = sb_ref[...]
    half = B_ROT // 2
    nslab = B_WIDTH // LANES
    lane = lax.broadcasted_iota(jnp.int32, cos.shape, 1)

    def block(i):
        ref, i = (xa_ref, i) if i < nblk else (xb_ref, i - nblk)
        x = ref[0, :, i * LANES:(i + 1) * LANES]
        return pltpu.roll(x, LANES - shift, 1) if shift else x

    nxt = block(0)
    for s in range(3 * nslab):
        if shift:
            cur, nxt = nxt, block(s + 1)
            x = jnp.where(lane < LANES - shift, cur, nxt)
        else:
            x = block(s)
        if s < 2 * nslab:
            x = x * cos + pltpu.roll(x, half, 1) * sin_a + pltpu.roll(x, LANES - half, 1) * sin_b
        if s < nslab:
            q_ref[0, :, s * LANES:(s + 1) * LANES] = (x * (B_QK ** -0.5)).astype(q_ref.dtype)
        elif s < 2 * nslab:
            k_ref[0, :, (s - nslab) * LANES:(s - nslab + 1) * LANES] = x
            k16_ref[0, :, (s - nslab) * LANES:(s - nslab + 1) * LANES] = x.astype(BF16)
        else:
            v_ref[0, :, (s - 2 * nslab) * LANES:(s - 2 * nslab + 1) * LANES] = x
            v16_ref[0, :, (s - 2 * nslab) * LANES:(s - 2 * nslab + 1) * LANES] = x.astype(BF16)


def rope_tables(pos):
    half = B_ROT // 2
    inv = ROPE_THETA ** (-jnp.arange(half, dtype=F32) * 2.0 / B_ROT)
    ang = pos.astype(F32)[:, None] * inv[None, :]
    cos, sin = jnp.cos(ang), jnp.sin(ang)
    n = pos.shape[0]
    one = jnp.ones((n, B_QK - B_ROT), F32)
    zero = jnp.zeros((n, B_QK - B_ROT), F32)
    zh = jnp.zeros((n, half), F32)
    comp = lambda a, b_, rest: jnp.concatenate([a, b_, rest], axis=1)
    cos_p = comp(cos, cos, one)
    sin_a = comp(zh, sin, zero)
    sin_b = comp(-sin, zh, zero)
    dup = lambda x: jnp.concatenate([x, x], axis=1)
    return dup(cos_p), dup(sin_a), dup(sin_b)


def rope_qkv(u, col0, tables, tm=256):
    bsz, seq_len, width = u.shape
    tm = min(tm, seq_len)
    if col0 == 0:
        nblk, first, shift = 3 * B_WIDTH // LANES, 0, 0
    else:
        nblk, first, shift = col0 // LANES, 1, col0 % LANES
        assert 2 * nblk * LANES >= shift + 3 * B_WIDTH + LANES and 3 * nblk * LANES <= width
    win = nblk * LANES
    tab = pl.BlockSpec((tm, LANES), lambda b, j: (j, 0))
    out_spec = pl.BlockSpec((1, tm, B_WIDTH), lambda b, j: (b, j, 0))
    sds = lambda dt: jax.ShapeDtypeStruct((bsz, seq_len, B_WIDTH), dt)
    return pl.pallas_call(
        functools.partial(_rope_body, shift=shift, nblk=nblk),
        grid=(bsz, seq_len // tm),
        in_specs=[pl.BlockSpec((1, tm, win), lambda b, j: (b, j, first)),
                  pl.BlockSpec((1, tm, win), lambda b, j: (b, j, first + 1 if shift else first)),
                  tab, tab, tab],
        out_specs=[out_spec] * 5,
        out_shape=[sds(BF16), sds(F32), sds(F32), sds(BF16), sds(BF16)],
        compiler_params=_cparams(("parallel", "parallel")),
        name="rope_qkv",
    )(u, u, *tables)


def _lam(lqk_ref, lam_init):
    d1 = jnp.sum(lqk_ref[0:1, :] * lqk_ref[1:2, :], axis=-1, keepdims=True)
    d2 = jnp.sum(lqk_ref[2:3, :] * lqk_ref[3:4, :], axis=-1, keepdims=True)
    return jnp.exp(d1) - jnp.exp(d2) + lam_init


def _diff_finish(acc1, l1, acc2, l2, lam, subln, lam_init):
    a = acc1 / l1 - lam * (acc2 / l2)
    y = a * lax.rsqrt(jnp.mean(a * a, axis=-1, keepdims=True) + EPS) * subln
    return y * (1.0 - lam_init)


_NT = (((1,), (1,)), ((), ()))


def _attn_prompt_body(q_ref, k_ref, v_ref, lqk_ref, sub_ref, o_ref, qs_ref, m_ref, l_ref, acc_ref,
                      *, tq, tk, hp, lam_init):
    qi = pl.program_id(2)
    ki = pl.program_id(3)
    nchain = 2 * hp
    head = lambda h: slice(h * B_VDIM, (h + 1) * B_VDIM)

    @pl.when(ki == 0)
    def _():
        lane = lax.broadcasted_iota(jnp.int32, (tq, B_VDIM), 1)
        for h in range(hp):
            qh = q_ref[0, :, head(h)]
            qs_ref[2 * h] = jnp.where(lane < B_QK, qh, jnp.zeros_like(qh))
            qs_ref[2 * h + 1] = jnp.where(lane >= B_QK, qh, jnp.zeros_like(qh))
        m_ref[...] = jnp.full(m_ref.shape, NEG_INF, F32)
        l_ref[...] = jnp.zeros(l_ref.shape, F32)
        acc_ref[...] = jnp.zeros(acc_ref.shape, F32)

    def block(masked):
        ks = [k_ref[0, :, head(h)] for h in range(hp)]
        vs = [v_ref[0, :, head(h)] for h in range(hp)]
        ss = [lax.dot_general(qs_ref[c], ks[c // 2], _NT, preferred_element_type=F32) for c in range(nchain)]
        if masked:
            row = qi * tq + lax.broadcasted_iota(jnp.int32, (tq, tk), 0)
            col = ki * tk + lax.broadcasted_iota(jnp.int32, (tq, tk), 1)
            keep = col <= row
            ss = [jnp.where(keep, s, NEG_INF) for s in ss]
        alphas, ps = [], []
        for c, s in enumerate(ss):
            m_prev = m_ref[c]
            m_new = jnp.maximum(m_prev, jnp.max(s, axis=-1, keepdims=True))
            alpha = jnp.exp(m_prev - m_new)
            p = jnp.exp(s - m_new)
            l_ref[c] = alpha * l_ref[c] + jnp.sum(p, axis=-1, keepdims=True)
            m_ref[c] = m_new
            alphas.append(alpha)
            ps.append(p.astype(BF16))
        for c in range(nchain):
            acc_ref[c] = alphas[c] * acc_ref[c] + jnp.dot(ps[c], vs[c // 2], preferred_element_type=F32)

    visible = ki * tk < (qi + 1) * tq
    full = ki * tk + tk - 1 <= qi * tq

    @pl.when(full)
    def _():
        block(False)

    @pl.when(visible & jnp.logical_not(full))
    def _():
        block(True)

    @pl.when(ki == pl.num_programs(3) - 1)
    def _():
        lam = _lam(lqk_ref, lam_init)
        for h in range(hp):
            y = _diff_finish(acc_ref[2 * h], l_ref[2 * h], acc_ref[2 * h + 1], l_ref[2 * h + 1], lam,
                             sub_ref[...], lam_init)
            o_ref[0, :, head(h)] = y.astype(o_ref.dtype)


def attn_prompt(q, k, v, lqk, subln, lam_init, tq=256, tk=1024, hp=2):
    bsz, seq_len, _ = q.shape
    tq = min(tq, seq_len)
    tk = min(tk, seq_len)
    assert seq_len % tq == 0 and seq_len % tk == 0 and B_HEADS % hp == 0
    nq, nk = seq_len // tq, seq_len // tk
    ngrp = B_HEADS // hp
    kv_blk = lambda qi, ki: jnp.minimum(ki, ((qi + 1) * tq - 1) // tk)
    wide = hp * B_VDIM
    return pl.pallas_call(
        functools.partial(_attn_prompt_body, tq=tq, tk=tk, hp=hp, lam_init=lam_init),
        grid=(bsz, ngrp, nq, nk),
        in_specs=[
            pl.BlockSpec((1, tq, wide), lambda b, g, qi, ki: (b, qi, g)),
            pl.BlockSpec((1, tk, wide), lambda b, g, qi, ki: (b, kv_blk(qi, ki), g)),
            pl.BlockSpec((1, tk, wide), lambda b, g, qi, ki: (b, kv_blk(qi, ki), g)),
            pl.BlockSpec((4, B_QK), lambda b, g, qi, ki: (0, 0)),
            pl.BlockSpec((1, B_VDIM), lambda b, g, qi, ki: (0, 0)),
        ],
        out_specs=pl.BlockSpec((1, tq, wide), lambda b, g, qi, ki: (b, qi, g)),
        out_shape=jax.ShapeDtypeStruct((bsz, seq_len, B_WIDTH), BF16),
        scratch_shapes=[pltpu.VMEM((2 * hp, tq, B_VDIM), BF16), pltpu.VMEM((2 * hp, tq, 1), F32),
                        pltpu.VMEM((2 * hp, tq, 1), F32), pltpu.VMEM((2 * hp, tq, B_VDIM), F32)],
        compiler_params=_cparams(("parallel", "parallel", "parallel", "arbitrary")),
        name="attn_prompt",
    )(q, k, v, lqk, subln.reshape(1, B_VDIM))


def _attn_decode_body(pt_ref, q_ref, kn_ref, vn_ref, ck0_ref, ck1_ref, cv0_ref, cv1_ref, lqk_ref, sub_ref, o_ref,
                      qa_ref, m_ref, l_ref, acc_ref, bias_ref, *, nq, n_steps, lam_init):
    del pt_ref
    p = pl.program_id(1)
    rows = 2 * nq
    nrow = B_HEADS * rows
    bf_round = lambda x: x.astype(BF16).astype(F32)

    @pl.when(p == 0)
    def _():
        q = q_ref[0].astype(F32)
        lane = lax.broadcasted_iota(jnp.int32, (nq, B_VDIM), 1)
        for h in range(B_HEADS):
            qh = q[:, h * B_VDIM:(h + 1) * B_VDIM]
            qa_ref[h * rows:(h + 1) * rows, :] = jnp.concatenate(
                [jnp.where(lane < B_QK, qh, 0.0), jnp.where(lane >= B_QK, qh, 0.0)], axis=0)
        m_ref[...] = jnp.full(m_ref.shape, NEG_INF, F32)
        l_ref[...] = jnp.zeros(l_ref.shape, F32)
        acc_ref[...] = jnp.zeros(acc_ref.shape, F32)
        own = (lax.broadcasted_iota(jnp.int32, bias_ref.shape, 0) // rows
               == lax.broadcasted_iota(jnp.int32, bias_ref.shape, 1) % B_HEADS)
        bias_ref[...] = jnp.where(own, 0.0, NEG_INF)

    @pl.when(p < n_steps)
    def _():
        half = ck0_ref.shape[2] * B_HEADS
        flat = lambda ref: ref[0, 0].reshape(half, B_VDIM).astype(BF16)
        k = jnp.concatenate([flat(ck0_ref), flat(ck1_ref)], axis=0)
        v = jnp.concatenate([flat(cv0_ref), flat(cv1_ref)], axis=0)
        s = lax.dot_general(qa_ref[...].astype(BF16), k, _NT, preferred_element_type=F32) + bias_ref[...]
        m_prev = m_ref[...]
        m_new = jnp.maximum(m_prev, jnp.max(s, axis=-1, keepdims=True))
        alpha = jnp.exp(m_prev - m_new)
        pr = jnp.exp(s - m_new)
        l_ref[...] = alpha * l_ref[...] + jnp.sum(pr, axis=-1, keepdims=True)
        acc_ref[...] = alpha * acc_ref[...] + jnp.dot(pr.astype(BF16), v, preferred_element_type=F32)
        m_ref[...] = m_new

    @pl.when(p == n_steps)
    def _():
        lam = _lam(lqk_ref, lam_init)
        qrow = lax.broadcasted_iota(jnp.int32, (nrow, 1), 0) % nq
        qa = bf_round(qa_ref[...])
        kn = bf_round(kn_ref[0])
        vn = bf_round(vn_ref[0])
        per_head = lambda x, j: jnp.concatenate(
            [jnp.broadcast_to(x[j:j + 1, h * B_VDIM:(h + 1) * B_VDIM], (rows, B_VDIM)) for h in range(B_HEADS)], axis=0)
        ss = [jnp.where(qrow >= j, jnp.sum(qa * per_head(kn, j), axis=-1, keepdims=True), NEG_INF)
              for j in range(nq)]
        m_prev = m_ref[...]
        m_new = m_prev
        for sj in ss:
            m_new = jnp.maximum(m_new, sj)
        alpha = jnp.exp(m_prev - m_new)
        l = alpha * l_ref[...]
        acc = alpha * acc_ref[...]
        for j, sj in enumerate(ss):
            pj = jnp.exp(sj - m_new)
            l = l + pj
            acc = acc + bf_round(pj) * per_head(vn, j)
        n = acc / l
        a = n - lam * pltpu.roll(n, nrow - nq, 0)
        y = a * lax.rsqrt(jnp.mean(a * a, axis=-1, keepdims=True) + EPS) * sub_ref[...] * (1.0 - lam_init)
        for h in range(B_HEADS):
            o_ref[0, :, h * B_VDIM:(h + 1) * B_VDIM] = y[h * rows:h * rows + nq].astype(o_ref.dtype)


def attn_decode(q, k, v, cache_k, cache_v, page_table, layer, lqk, subln, lam_init):
    bsz, nq, _ = q.shape
    n_pages = page_table.shape[1]
    page = cache_k.shape[2]
    nrow = B_HEADS * 2 * nq
    assert n_pages % 2 == 0
    n_steps = n_pages // 2
    cache_spec = lambda odd: pl.BlockSpec(
        (1, 1, page, B_HEADS, B_VDIM),
        lambda b, p, pt: (layer, pt[b * n_pages + 2 * jnp.minimum(p, n_steps - 1) + odd], 0, 0, 0))
    grid_spec = pltpu.PrefetchScalarGridSpec(
        num_scalar_prefetch=1,
        grid=(bsz, n_steps + 1),
        in_specs=[
            pl.BlockSpec((1, nq, B_WIDTH), lambda b, p, pt: (b, 0, 0)),
            pl.BlockSpec((1, nq, B_WIDTH), lambda b, p, pt: (b, 0, 0)),
            pl.BlockSpec((1, nq, B_WIDTH), lambda b, p, pt: (b, 0, 0)),
            cache_spec(0), cache_spec(1), cache_spec(0), cache_spec(1),
            pl.BlockSpec((4, B_QK), lambda b, p, pt: (0, 0)),
            pl.BlockSpec((1, B_VDIM), lambda b, p, pt: (0, 0)),
        ],
        out_specs=pl.BlockSpec((1, nq, B_WIDTH), lambda b, p, pt: (b, 0, 0)),
        scratch_shapes=[
            pltpu.VMEM((nrow, B_VDIM), F32),
            pltpu.VMEM((nrow, 1), F32),
            pltpu.VMEM((nrow, 1), F32),
            pltpu.VMEM((nrow, B_VDIM), F32),
            pltpu.VMEM((nrow, 2 * page * B_HEADS), F32),
        ],
    )
    return pl.pallas_call(
        functools.partial(_attn_decode_body, nq=nq, n_steps=n_steps, lam_init=lam_init),
        grid_spec=grid_spec,
        out_shape=jax.ShapeDtypeStruct((bsz, nq, B_WIDTH), F32),
        compiler_params=_cparams(("parallel", "arbitrary")),
        name="attn_decode",
    )(page_table.reshape(-1), q, k, v, cache_k, cache_k, cache_v, cache_v, lqk, subln.reshape(1, B_VDIM)).astype(BF16)


_TN = (((0,), (0,)), ((), ()))


def _hgrn_body(qa_ref, qb_ref, fa_ref, fb_ref, ia_ref, ib_ref, ga_ref, gb_ref, lb_ref, cn_ref, s0_ref,
               o_ref, sfin_ref, s_ref, *, tc, n_valid, shift):
    c = pl.program_id(2)
    lane_in = lax.broadcasted_iota(jnp.int32, (tc, LANES), 1)

    def slab(a_ref, b_ref):
        if not shift:
            return a_ref[0]
        return jnp.where(lane_in < LANES - shift, pltpu.roll(a_ref[0], LANES - shift, 1),
                         pltpu.roll(b_ref[0], LANES - shift, 1))

    @pl.when(c == 0)
    def _():
        s_ref[...] = s0_ref[0, 0]

    nrow = -(-tc // C_CHUNK) * C_CHUNK
    pad = lambda x: x if nrow == tc else jnp.concatenate([x, jnp.zeros((nrow - tc, x.shape[1]), F32)], axis=0)
    q = _silu(pad(slab(qa_ref, qb_ref)))
    lb = lb_ref[...]
    fs = lb + (1.0 - lb) * _sigmoid(pad(slab(fa_ref, fb_ref)))
    kin = 1.0 - fs
    gl = jnp.log(fs)
    iv = pad(slab(ia_ref, ib_ref))
    row = lax.broadcasted_iota(jnp.int32, (nrow, C_DK), 0)
    if n_valid < nrow:
        kin = jnp.where(row < n_valid, kin, 0.0)
        gl = jnp.where(row < n_valid, gl, 0.0)
    b = gl
    sh = 1
    while sh < C_CHUNK:
        b = b + jnp.where(row % C_CHUNK >= sh, pltpu.roll(b, sh, 0), 0.0)
        sh *= 2
    trow = lax.broadcasted_iota(jnp.int32, (C_CHUNK, C_DK), 0)
    eye = (lax.broadcasted_iota(jnp.int32, (C_DK, C_DK), 0)
           == lax.broadcasted_iota(jnp.int32, (C_DK, C_DK), 1)).astype(F32)
    outs = []
    for ch in range(nrow // C_CHUNK):
        rs = slice(ch * C_CHUNK, (ch + 1) * C_CHUNK)
        bc, qc, kc, ic = b[rs], q[rs], kin[rs], iv[rs]
        s = s_ref[...]
        o = jnp.dot((qc * jnp.exp(bc)).astype(BF16), s.astype(BF16), preferred_element_type=F32)
        for sidx in range(C_CHUNK):
            dec = jnp.exp(jnp.where(trow >= sidx, bc - bc[sidx:sidx + 1, :], NEG_INF))
            att = jnp.sum(qc * kc[sidx:sidx + 1, :] * dec, axis=-1, keepdims=True)
            o = o + att * ic[sidx:sidx + 1, :]
        bl = bc[C_CHUNK - 1:C_CHUNK, :]
        kd = kc * jnp.exp(bl - bc)
        inc = lax.dot_general(kd.astype(BF16), ic.astype(BF16), _TN, preferred_element_type=F32)
        ecol = jnp.sum(eye * jnp.exp(bl), axis=1, keepdims=True)
        s_ref[...] = s * ecol + inc
        outs.append(o)
    o = jnp.concatenate(outs, axis=0)[0:tc]
    y = o * lax.rsqrt(jnp.mean(o * o, axis=-1, keepdims=True) + EPS) * cn_ref[...]
    o_ref[0] = (y * _silu(slab(ga_ref, gb_ref))).astype(o_ref.dtype)

    @pl.when(c == pl.num_programs(2) - 1)
    def _():
        sfin_ref[0, 0] = s_ref[...]


def hgrn(u, col0, s0, lb, c_norm, tc=256):
    bsz, seq_len, _ = u.shape
    tc = min(tc, seq_len)
    assert seq_len % tc == 0
    blk0, shift = col0 // LANES, col0 % LANES
    tok = lambda off: pl.BlockSpec((1, tc, LANES), lambda b, h, c: (b, c, blk0 + off + h))
    pair = lambda grp: [tok(grp * C_HEADS), tok(grp * C_HEADS + (1 if shift else 0))]
    st = pl.BlockSpec((1, 1, C_DK, C_DV), lambda b, h, c: (b, h, 0, 0))
    return pl.pallas_call(
        functools.partial(_hgrn_body, tc=tc, n_valid=tc, shift=shift),
        grid=(bsz, C_HEADS, seq_len // tc),
        in_specs=pair(0) + pair(1) + pair(2) + pair(3) + [
            pl.BlockSpec((1, LANES), lambda b, h, c: (0, h)),
            pl.BlockSpec((1, C_DV), lambda b, h, c: (0, 0)),
            st],
        out_specs=[pl.BlockSpec((1, tc, LANES), lambda b, h, c: (b, c, h)), st],
        out_shape=[jax.ShapeDtypeStruct((bsz, seq_len, C_WIDTH), BF16),
                   jax.ShapeDtypeStruct((bsz, C_HEADS, C_DK, C_DV), F32)],
        scratch_shapes=[pltpu.VMEM((C_DK, C_DV), F32)],
        compiler_params=_cparams(("parallel", "parallel", "arbitrary")),
        name="hgrn",
    )(*([u] * 8), lb.reshape(1, C_FK), c_norm.reshape(1, C_DV), s0)


def _pair_state(s):
    bsz = s.shape[0]
    return s.reshape(bsz, A_PAIRS, 2, A_HEAD, A_HEAD).transpose(0, 1, 3, 2, 4).reshape(bsz, A_PAIRS, A_HEAD, LANES)


def _unpair_state(s):
    bsz = s.shape[0]
    return s.reshape(bsz, A_PAIRS, A_HEAD, 2, A_HEAD).transpose(0, 1, 3, 2, 4).reshape(bsz, A_HEADS, A_HEAD, A_HEAD)


def _layer_params(l, P):
    padc = A_COLS_PAD - A_COLS
    w2 = jnp.zeros((LANES, 2 * A_WIDTH), F32)
    w2 = w2.at[0:A_DECAY_R, 0:A_WIDTH].set(P["a_w_up"][l])
    w2 = w2.at[A_DECAY_R:A_DECAY_R + A_ICLR_R, A_WIDTH:].set(P["a_a_up"][l])
    return dict(
        mu=jnp.pad(P["a_mu"][l], (0, padc)),
        w2=w2,
        gup=jnp.pad(P["a_g_up"][l], ((0, 2 * LANES - A_GATE_R), (0, 0))),
        lqk=jnp.stack([P["b_lq1"][l], P["b_lk1"][l], P["b_lq2"][l], P["b_lk2"][l]]),
        r_k=P["a_r_k"][l].reshape(A_WIDTH),
    )


def _run_trunk(x, pos, caches, wkv0, shift0, hgrn0, conv0, lbs, P, LP):
    bsz, seq_len, d = x.shape
    t = bsz * seq_len
    h = x.reshape(t, d)
    tables = rope_tables(pos)
    padc = A_COLS_PAD - A_COLS
    ks, vs, wkvs, shifts, hgrns, convs = [], [], [], [], [], []
    for l in range(DEPTH):
        lp = LP[l]
        hn = rmsnorm_rows(h, P["norm_mix"][l], BF16)
        u = matmul(hn, P["w_in"], layer=l, pad_n=True, tn=512)
        u = u.reshape(bsz, seq_len, u.shape[-1])
        r, dcy, kp, v, kk, bb, g, bonus = rwkv_pre(
            u, jnp.pad(shift0[l], ((0, 0), (0, padc))), lp["mu"], lp["w2"], lp["gup"],
            P["a_w0"][l], P["a_a0"][l], P["a_k_k"][l], P["a_k_a"][l], lp["r_k"])
        o_a, s_a = rwkv_scan(r, dcy, kp, v, kk, bb, _pair_state(wkv0[l]))
        flat = lambda z: z.reshape(t, A_WIDTH)
        oa = rwkv_post(flat(o_a), flat(bonus), flat(g), P["a_ln_w"][l], P["a_ln_b"][l])
        lam_init = 0.8 - 0.6 * math.exp(-0.3 * l)
        qb, kb, vb, kb16, vb16 = rope_qkv(u, A_COLS, tables)
        if caches is None:
            ob = attn_prompt(qb, kb16, vb16, lp["lqk"], P["b_subln"][l], lam_init)
        else:
            ob = attn_decode(qb, kb, vb, caches[0], caches[1], caches[2], l, lp["lqk"], P["b_subln"][l], lam_init)
        oc, s_c = hgrn(u, A_COLS + 3 * B_WIDTH, hgrn0[l], lbs[l], P["c_norm"][l])
        mix = jnp.concatenate([oa, ob.reshape(t, B_WIDTH), oc.reshape(t, C_WIDTH)], axis=-1)
        h = matmul(mix, P["w_out"], layer=l, res=h)
        hf = rmsnorm_rows(h, P["norm_ffn"][l], BF16)
        mid, buf = ffn_up(hf, P["f_gate"], P["f_up"], P["f_conv"], P["f_conv_b"], conv0[l], seq_len, layer=l)
        h = matmul_kloop(mid, P["f_down"], l, h)
        ks.append(kb.reshape(bsz, seq_len, B_HEADS, B_VDIM))
        vs.append(vb.reshape(bsz, seq_len, B_HEADS, B_VDIM))
        wkvs.append(_unpair_state(s_a))
        shifts.append(u[:, -1, :A_COLS])
        hgrns.append(s_c)
        convs.append(buf)
    y = rmsnorm_rows(h, P["norm_final"], F32).reshape(bsz, seq_len, d)
    st = jnp.stack
    return y, st(ks), st(vs), st(wkvs), st(shifts), st(hgrns), st(convs)


def kernel(x_prompt, x_sample, cache_k, cache_v, page_table, state_wkv, state_shift, state_hgrn, state_conv,
           norm_mix, w_in, a_mu, a_w0, a_w_up, a_a0, a_a_up, a_g_up, a_k_k, a_k_a, a_r_k, a_ln_w, a_ln_b,
           b_lq1, b_lk1, b_lq2, b_lk2, b_subln, c_lb_logits, c_norm, w_out, norm_ffn, f_gate, f_up,
           f_conv, f_conv_b, f_down, norm_final):
    P = dict(norm_mix=norm_mix, w_in=w_in, a_mu=a_mu, a_w0=a_w0, a_w_up=a_w_up, a_a0=a_a0, a_a_up=a_a_up,
             a_g_up=a_g_up, a_k_k=a_k_k, a_k_a=a_k_a, a_r_k=a_r_k, a_ln_w=a_ln_w, a_ln_b=a_ln_b,
             b_lq1=b_lq1, b_lk1=b_lk1, b_lq2=b_lq2, b_lk2=b_lk2, b_subln=b_subln, c_norm=c_norm,
             w_out=w_out, norm_ffn=norm_ffn, f_gate=f_gate, f_up=f_up, f_conv=f_conv, f_conv_b=f_conv_b,
             f_down=f_down, norm_final=norm_final)
    LP = [_layer_params(l, P) for l in range(DEPTH)]
    lbs = lower_bounds(c_lb_logits)
    bp, lp_ = x_prompt.shape[:2]
    past_len = page_table.shape[1] * cache_k.shape[2]
    pos_p = jnp.arange(lp_)
    pos_s = past_len + jnp.arange(x_sample.shape[1])
    zeros = lambda *s: jnp.zeros(s, F32)
    out_p = _run_trunk(
        x_prompt, pos_p, None,
        zeros(DEPTH, bp, A_HEADS, A_HEAD, A_HEAD), zeros(DEPTH, bp, A_COLS),
        zeros(DEPTH, bp, C_HEADS, C_DK, C_DV), zeros(DEPTH, bp, CONV_W - 1, D_FF), lbs, P, LP)
    caches = (cache_k, cache_v, page_table)
    out_s = _run_trunk(x_sample, pos_s, caches, state_wkv, state_shift, state_hgrn, state_conv, lbs, P, LP)
    y_p, k_p, v_p, wkv_p, sh_p, hg_p, cv_p = out_p
    y_s, k_s, v_s, wkv_s, sh_s, hg_s, cv_s = out_s
    return (y_p, y_s, k_p, v_p, k_s, v_s, wkv_p, wkv_s, sh_p, sh_s, hg_p, hg_s, cv_p, cv_s)
```

```python
import functools
import math

import jax
import jax.numpy as jnp
from jax import lax
from jax.experimental import pallas as pl
from jax.experimental.pallas import tpu as pltpu

F32 = jnp.float32
BF16 = jnp.bfloat16

V7X_VMEM_BYTES = 64 * 1024 * 1024
VMEM_LIMIT = V7X_VMEM_BYTES - 8 * 1024 * 1024
LANES = 128
SUBLANES = 8

D_MODEL = 4096
DEPTH = 4
A_WIDTH = D_MODEL // 4
A_HEAD = 64
A_HEADS = A_WIDTH // A_HEAD
A_PAIRS = A_HEADS // 2
A_DECAY_R = 64
A_ICLR_R = 64
A_GATE_R = 160
A_COLS = 3 * A_WIDTH + A_DECAY_R + A_ICLR_R + A_GATE_R
A_COLS_PAD = 3 * A_WIDTH + 3 * LANES
A_GN_EPS = 64e-5
B_WIDTH = D_MODEL // 2
B_HEADS = 16
B_VDIM = B_WIDTH // B_HEADS
B_QK = B_VDIM // 2
B_ROT = B_QK // 4
ROPE_THETA = 500000.0
C_WIDTH = D_MODEL - A_WIDTH - B_WIDTH
C_HEADS = 8
C_DK = 128
C_DV = C_WIDTH // C_HEADS
C_FK = C_HEADS * C_DK
C_CHUNK = 16
D_FF = 11008
CONV_W = 3
EPS = 1e-5
NEG_INF = float("-inf")


def _cparams(sem):
    return pltpu.CompilerParams(dimension_semantics=sem, vmem_limit_bytes=VMEM_LIMIT)


def _sigmoid(x):
    return 1.0 / (1.0 + jnp.exp(-x))


def _silu(x):
    return x * _sigmoid(x)


def _block_ones(dtype):
    r = lax.broadcasted_iota(jnp.int32, (LANES, LANES), 0) // A_HEAD
    c = lax.broadcasted_iota(jnp.int32, (LANES, LANES), 1) // A_HEAD
    return (r == c).astype(dtype)


def _rmsnorm_body(x_ref, w_ref, o_ref):
    x = x_ref[...]
    y = x * lax.rsqrt(jnp.mean(x * x, axis=-1, keepdims=True) + EPS)
    o_ref[...] = (y * w_ref[...]).astype(o_ref.dtype)


def rmsnorm_rows(x, w, out_dtype, tr=256):
    t, d = x.shape
    tr = min(tr, t)
    return pl.pallas_call(
        _rmsnorm_body,
        grid=(pl.cdiv(t, tr),),
        in_specs=[pl.BlockSpec((tr, d), lambda i: (i, 0)), pl.BlockSpec((1, d), lambda i: (0, 0))],
        out_specs=pl.BlockSpec((tr, d), lambda i: (i, 0)),
        out_shape=jax.ShapeDtypeStruct((t, d), out_dtype),
        compiler_params=_cparams(("parallel",)),
        name="rmsnorm_rows",
    )(x, w.reshape(1, d))


def _mm_body(*refs, nk, tk, k_total, n_total, has_res, w_rows_are_n=False):
    if has_res:
        a_ref, w_ref, r_ref, o_ref = refs
    else:
        a_ref, w_ref, o_ref = refs
        r_ref = None
    a = a_ref[...]
    w = w_ref[...].astype(BF16)
    tn = w.shape[0] if w_rows_are_n else w.shape[1]
    if n_total % tn:
        valid = n_total - pl.program_id(1) * tn
        n_iota = (lax.broadcasted_iota(jnp.int32, (tn, 1), 0) if w_rows_are_n
                  else lax.broadcasted_iota(jnp.int32, (1, tn), 1))
        w = jnp.where(n_iota < valid, w, jnp.zeros_like(w))
    if nk == 1:
        if w_rows_are_n:
            acc = lax.dot_general(a, w, (((1,), (1,)), ((), ())), preferred_element_type=F32)
        else:
            acc = jnp.dot(a, w, preferred_element_type=F32)
        if has_res:
            acc = acc + r_ref[...]
        o_ref[...] = acc.astype(o_ref.dtype)
        return
    k = pl.program_id(2)
    if k_total % tk:
        valid = k_total - k * tk
        a = jnp.where(lax.broadcasted_iota(jnp.int32, a.shape, 1) < valid, a, jnp.zeros_like(a))
        w = jnp.where(lax.broadcasted_iota(jnp.int32, w.shape, 0) < valid, w, jnp.zeros_like(w))
    part = jnp.dot(a, w, preferred_element_type=F32)

    @pl.when(k == 0)
    def _():
        o_ref[...] = (part + r_ref[...]) if has_res else part

    @pl.when(k > 0)
    def _():
        o_ref[...] += part


def matmul(a, w, layer=None, res=None, out_dtype=F32, tm=2048, tn=256, tk=None, pad_n=False, w_rows_are_n=False):
    m, kdim = a.shape
    n = w.shape[-2] if w_rows_are_n else w.shape[-1]
    tm = min(tm, m)
    tk = kdim if tk is None else tk
    nk = pl.cdiv(kdim, tk)
    if nk > 1:
        assert out_dtype == F32
    grid = (pl.cdiv(m, tm), pl.cdiv(n, tn), nk)
    n_out = grid[1] * tn if pad_n else n
    once = dict(pipeline_mode=pl.Buffered(1))
    a_kw = once if (nk == 1 and grid[1] > 1) else {}
    if layer is None:
        w_spec = pl.BlockSpec((tk, tn), lambda i, j, k: (k, j))
    else:
        w_spec = pl.BlockSpec((None, tk, tn), lambda i, j, k: (layer, k, j))
    if w_rows_are_n:
        assert layer is not None and nk == 1
        w_spec = pl.BlockSpec((None, tn, tk), lambda i, j, k: (layer, j, k))
    in_specs = [pl.BlockSpec((tm, tk), lambda i, j, k: (i, k), **a_kw), w_spec]
    args = [a, w]
    if res is not None:
        in_specs.append(pl.BlockSpec((tm, tn), lambda i, j, k: (i, j), **(once if nk > 1 else {})))
        args.append(res)
    return pl.pallas_call(
        functools.partial(_mm_body, nk=nk, tk=tk, k_total=kdim, n_total=n, has_res=res is not None,
                          w_rows_are_n=w_rows_are_n),
        grid=grid,
        in_specs=in_specs,
        out_specs=pl.BlockSpec((tm, tn), lambda i, j, k: (i, j)),
        out_shape=jax.ShapeDtypeStruct((m, n_out), out_dtype),
        compiler_params=_cparams(("parallel", "parallel", "arbitrary")),
        name="matmul",
    )(*args)


def _mm_kloop_body(a_ref, w_ref, r_ref, o_ref, *, tk):
    k_total = a_ref.shape[1]
    acc = r_ref[...]
    for k0 in range(0, k_total, tk):
        kw = min(tk, k_total - k0)
        acc = acc + jnp.dot(a_ref[:, k0:k0 + kw], w_ref[k0:k0 + kw, :].astype(BF16), preferred_element_type=F32)
    o_ref[...] = acc


def matmul_kloop(a, w, layer, res, tm=1024, tn=256, tk=1024):
    m, kdim = a.shape
    n = w.shape[-1]
    tm = min(tm, m)
    assert m % tm == 0 and n % tn == 0
    return pl.pallas_call(
        functools.partial(_mm_kloop_body, tk=tk),
        grid=(m // tm, n // tn),
        in_specs=[pl.BlockSpec((tm, kdim), lambda i, j: (i, 0), pipeline_mode=pl.Buffered(1)),
                  pl.BlockSpec((None, kdim, tn), lambda i, j: (layer, 0, j)),
                  pl.BlockSpec((tm, tn), lambda i, j: (i, j))],
        out_specs=pl.BlockSpec((tm, tn), lambda i, j: (i, j)),
        out_shape=jax.ShapeDtypeStruct((m, n), F32),
        compiler_params=_cparams(("parallel", "parallel")),
        name="matmul_kloop",
    )(a, w, res)


def _ffn_up_body(a_ref, wg_ref, wu_ref, cw_ref, cb_ref, buf_ref, o_ref, nbuf_ref, ext_ref, *, nb, seq_len):
    tm = nb * seq_len
    a = a_ref[...]
    gate = jnp.dot(a, wg_ref[...].astype(BF16), preferred_element_type=F32)
    up = jnp.dot(a, wu_ref[...].astype(BF16), preferred_element_type=F32)
    ext_ref[0:SUBLANES, :] = jnp.zeros((SUBLANES, gate.shape[1]), F32)
    ext_ref[SUBLANES:SUBLANES + tm, :] = gate
    g1 = ext_ref[SUBLANES - 1:SUBLANES - 1 + tm, :]
    g2 = ext_ref[SUBLANES - 2:SUBLANES - 2 + tm, :]
    row = lax.broadcasted_iota(jnp.int32, gate.shape, 0)
    for s in range(nb):
        b0 = buf_ref[s, 0:1, :]
        b1 = buf_ref[s, 1:2, :]
        g1 = jnp.where(row == s * seq_len, b1, g1)
        g2 = jnp.where(row == s * seq_len, b0, g2)
        g2 = jnp.where(row == s * seq_len + 1, b1, g2)
        last = SUBLANES + (s + 1) * seq_len
        nbuf_ref[s, :, :] = ext_ref[last - 2:last, :]
    conv = cb_ref[...] + g2 * cw_ref[0:1, :] + g1 * cw_ref[1:2, :] + gate * cw_ref[2:3, :]
    o_ref[...] = (_silu(conv) * up).astype(o_ref.dtype)


def ffn_up(hn, w_gate, w_up, w_conv, b_conv, buf, seq_len, layer=None, tn=256):
    t, d = hn.shape
    n = w_gate.shape[-1]
    nseq = buf.shape[0]
    nb = 1 if seq_len >= 256 else nseq
    tm = nb * seq_len
    assert seq_len >= 2 and t == nseq * seq_len and n % tn == 0
    grid = (t // tm, n // tn)
    if layer is None:
        wspec = lambda rows: pl.BlockSpec((rows, tn), lambda i, j: (0, j))
        b_conv = b_conv.reshape(1, n)
    else:
        wspec = lambda rows: pl.BlockSpec((None, rows, tn), lambda i, j: (layer, 0, j))
        b_conv = b_conv.reshape(-1, 1, n)
    return pl.pallas_call(
        functools.partial(_ffn_up_body, nb=nb, seq_len=seq_len),
        grid=grid,
        in_specs=[
            pl.BlockSpec((tm, d), lambda i, j: (i, 0), pipeline_mode=pl.Buffered(1)),
            wspec(d), wspec(d), wspec(CONV_W), wspec(1),
            pl.BlockSpec((nb, CONV_W - 1, tn), lambda i, j: (i, 0, j)),
        ],
        out_specs=[
            pl.BlockSpec((tm, tn), lambda i, j: (i, j)),
            pl.BlockSpec((nb, CONV_W - 1, tn), lambda i, j: (i, 0, j)),
        ],
        out_shape=[
            jax.ShapeDtypeStruct((t, n), BF16),
            jax.ShapeDtypeStruct((nseq, CONV_W - 1, n), F32),
        ],
        scratch_shapes=[pltpu.VMEM((SUBLANES + tm, tn), F32)],
        compiler_params=_cparams(("parallel", "parallel")),
        name="ffn_up",
    )(hn, w_gate, w_up, w_conv, b_conv, buf)


def _lower_bounds_body(x_ref, o_ref):
    x = x_ref[...]
    e = jnp.exp(x - jnp.max(x, axis=0, keepdims=True))
    p = e / jnp.sum(e, axis=0, keepdims=True)
    run = jnp.zeros_like(p[0:1, :])
    for l in range(x.shape[0]):
        run = run + p[l:l + 1, :]
        o_ref[l:l + 1, :] = run - p[0:1, :]


def lower_bounds(logits):
    return pl.pallas_call(
        _lower_bounds_body,
        out_shape=jax.ShapeDtypeStruct(logits.shape, F32),
        name="lower_bounds",
    )(logits)


def _seg_sum(x, ones):
    parts = []
    for s in range(x.shape[1] // LANES):
        parts.append(jnp.dot(x[:, s * LANES:(s + 1) * LANES], ones,
                             preferred_element_type=F32, precision=lax.Precision.HIGHEST))
    return jnp.concatenate(parts, axis=1)


def _rwkv_pre_body(u_ref, prev_ref, mu_ref, w2_ref, gup_ref, w0_ref, a0_ref, kk_ref, ka_ref, rk_ref,
                   r_o, d_o, k_o, v_o, kk_o, b_o, g_o, bonus_o, ext_ref, *, tm):
    j = pl.program_id(1)
    u = u_ref[0]

    @pl.when(j == 0)
    def _():
        ext_ref[SUBLANES - 1:SUBLANES, :] = prev_ref[0]

    ext_ref[SUBLANES:SUBLANES + tm, :] = u
    shifted = ext_ref[SUBLANES - 1:SUBLANES - 1 + tm, :]
    ext_ref[SUBLANES - 1:SUBLANES, :] = u[tm - 1:tm, :]
    xm = u + (shifted - u) * mu_ref[...]
    w = A_WIDTH
    r = xm[:, 0:w]
    k = xm[:, w:2 * w]
    v = xm[:, 2 * w:3 * w]
    lr = xm[:, 3 * w:3 * w + LANES]
    gd = xm[:, 3 * w + LANES:3 * w + 3 * LANES]
    lane = lax.broadcasted_iota(jnp.int32, lr.shape, 1)
    lr = jnp.where(lane < A_DECAY_R, jnp.tanh(lr), lr)
    wa = jnp.dot(lr.astype(BF16), w2_ref[...].astype(BF16), preferred_element_type=F32)
    wraw = -(w0_ref[...] + wa[:, 0:w])
    softplus = jnp.maximum(wraw, 0.0) + jnp.log(1.0 + jnp.exp(-jnp.abs(wraw)))
    decay = jnp.exp(-jnp.exp(-softplus - 0.5))
    a = _sigmoid(a0_ref[...] + wa[:, w:2 * w])
    g = jnp.dot(_sigmoid(gd).astype(BF16), gup_ref[...].astype(BF16), preferred_element_type=F32)
    ones = _block_ones(F32)
    kk = k * kk_ref[...]
    norm = jnp.sqrt(_seg_sum(kk * kk, ones))
    kk = kk / jnp.maximum(norm, 1e-12)
    kp = k * (1.0 + (a - 1.0) * ka_ref[...])
    bonus = _seg_sum(r * kp * rk_ref[...], ones) * v
    r_o[0] = r
    d_o[0] = decay
    k_o[0] = kp
    v_o[0] = v
    kk_o[0] = kk
    b_o[0] = kk * a
    g_o[0] = g
    bonus_o[0] = bonus


def rwkv_pre(ua, prev, mu, w2, gup, w0, a0, k_k, k_a, r_k, tm=256):
    bsz, seq_len, width = ua.shape
    cols = A_COLS_PAD
    tm = min(tm, seq_len)
    assert seq_len % tm == 0 and width % cols == 0
    w = A_WIDTH
    row = lambda x: x.reshape(1, -1)
    vec = lambda n: pl.BlockSpec((1, n), lambda b, j: (0, 0))
    out_spec = pl.BlockSpec((1, tm, w), lambda b, j: (b, j, 0))
    out_sds = jax.ShapeDtypeStruct((bsz, seq_len, w), F32)
    return pl.pallas_call(
        functools.partial(_rwkv_pre_body, tm=tm),
        grid=(bsz, seq_len // tm),
        in_specs=[
            pl.BlockSpec((1, tm, cols), lambda b, j: (b, j, 0)),
            pl.BlockSpec((1, 1, cols), lambda b, j: (b, 0, 0)),
            vec(cols),
            pl.BlockSpec((LANES, 2 * w), lambda b, j: (0, 0)),
            pl.BlockSpec((2 * LANES, w), lambda b, j: (0, 0)),
            vec(w), vec(w), vec(w), vec(w), vec(w),
        ],
        out_specs=[out_spec] * 8,
        out_shape=[out_sds] * 8,
        scratch_shapes=[pltpu.VMEM((SUBLANES + tm, cols), F32)],
        compiler_params=_cparams(("parallel", "arbitrary")),
        name="rwkv_pre",
    )(ua, prev.reshape(bsz, 1, cols), row(mu), w2, gup, row(w0), row(a0), row(k_k), row(k_a), row(r_k))


def _rwkv_scan_body(r_ref, d_ref, k_ref, v_ref, kk_ref, b_ref, s0_ref, o_ref, sfin_ref, s_ref, p2_ref, *, nbt, tc):
    c = pl.program_id(1)

    @pl.when(c == 0)
    def _():
        s_ref[...] = s0_ref[...]

    sub = min(SUBLANES, tc)
    npair = nbt * A_PAIRS
    ones = _block_ones(BF16)
    ones2 = jnp.concatenate([ones, ones], axis=0)
    ones4 = (lax.broadcasted_iota(jnp.int32, (2 * LANES, 2 * LANES), 0) // A_HEAD
             == lax.broadcasted_iota(jnp.int32, (2 * LANES, 2 * LANES), 1) // A_HEAD).astype(BF16)
    eye2 =(lax.broadcasted_iota(jnp.int32, (A_HEAD, LANES), 0)
            == lax.broadcasted_iota(jnp.int32, (A_HEAD, LANES), 1) % A_HEAD)
    eye2b = jnp.where(eye2, 1.0, 0.0).astype(BF16)
    wr = lax.broadcasted_iota(jnp.int32, (2 * sub, sub * LANES), 0)
    wc = lax.broadcasted_iota(jnp.int32, (2 * sub, sub * LANES), 1)
    wsel = jnp.where((wc // LANES == wr % sub) & ((wc % LANES) // A_HEAD == wr // sub), 1.0, 0.0).astype(BF16)
    bc = lambda x: jnp.broadcast_to(x, (A_HEAD, LANES))
    prs = [divmod(q, A_PAIRS) for q in range(npair)]

    nhalf = 2 if npair % 4 == 0 else 1
    per = npair // nhalf

    def group(gi, carry):
        rows = pl.ds(pl.multiple_of(gi * sub, sub), sub)
        tile = lambda ref, q: ref[prs[q][0], rows, pl.ds(prs[q][1] * LANES, LANES)]
        row = lambda x8, j: x8[j:j + 1, :]

        def issue(half, j):
            p1s, p3s = [], []
            for q in range(half * per, (half + 1) * per):
                nb, p = prs[q]
                v8 = tile(v_ref, q)
                v_hi8 = v8.astype(BF16)
                v_lo8 = (v8 - v_hi8.astype(F32)).astype(BF16)
                p1s.append((s_ref[nb, p] * bc(row(tile(kk_ref, q), j))).astype(BF16))
                p3s.append(jnp.concatenate([eye2b * bc(row(v_hi8, j)), eye2b * bc(row(v_lo8, j))], axis=1))
            if per % 2 == 0:
                lhs = jnp.concatenate([jnp.concatenate(p1s[i:i + 2], axis=1) for i in range(0, per, 2)], axis=0)
                sa = jnp.dot(lhs, ones4, preferred_element_type=F32)
                sas = [sa[(i // 2) * A_HEAD:(i // 2 + 1) * A_HEAD, (i % 2) * LANES:(i % 2 + 1) * LANES]
                       for i in range(per)]
            else:
                sa = jnp.dot(jnp.concatenate(p1s, axis=0), ones, preferred_element_type=F32)
                sas = [sa[i * A_HEAD:(i + 1) * A_HEAD] for i in range(per)]
            vc = jnp.dot(jnp.concatenate(p3s, axis=0), ones2, preferred_element_type=F32)
            return sas, [vc[i * A_HEAD:(i + 1) * A_HEAD] for i in range(per)]

        def update(half, j, sas, vcs):
            for i, q in enumerate(range(half * per, (half + 1) * per)):
                nb, p = prs[q]
                s = (s_ref[nb, p] * bc(row(tile(d_ref, q), j)) - sas[i] * bc(row(tile(b_ref, q), j))
                     + vcs[i] * bc(row(tile(k_ref, q), j)))
                s_ref[nb, p] = s
                p2_ref[q * A_HEAD:(q + 1) * A_HEAD, j * LANES:(j + 1) * LANES] = (
                    s * bc(row(tile(r_ref, q), j))).astype(BF16)

        pend = [issue(0, 0)] + [None] * (nhalf - 1)
        for j in range(sub):
            for half in range(1, nhalf):
                pend[half] = issue(half, j)
            update(0, j, *pend[0])
            if j + 1 < sub:
                pend[0] = issue(0, j + 1)
            for half in range(1, nhalf):
                update(half, j, *pend[half])
        for q, (nb, p) in enumerate(prs):
            qs = slice(q * A_HEAD, (q + 1) * A_HEAD)
            ot = lax.dot_general(wsel, p2_ref[qs, :], _NT, preferred_element_type=F32)
            o_ref[nb, rows, pl.ds(p * LANES, LANES)] = jnp.concatenate([ot[0:sub], ot[sub:2 * sub]], axis=1)
        return carry

    lax.fori_loop(0, tc // sub, group, 0)

    @pl.when(c == pl.num_programs(1) - 1)
    def _():
        sfin_ref[...] = s_ref[...]


def rwkv_scan(r, d, k, v, kk, b, s0, tc=64, nbt=4):
    bsz, seq_len, w = r.shape
    tc = min(tc, seq_len)
    nbt = min(nbt, bsz)
    assert seq_len % tc == 0 and bsz % nbt == 0 and w == A_WIDTH
    tok = pl.BlockSpec((nbt, tc, w), lambda i, c: (i, c, 0))
    st = pl.BlockSpec((nbt, A_PAIRS, A_HEAD, LANES), lambda i, c: (i, 0, 0, 0))
    return pl.pallas_call(
        functools.partial(_rwkv_scan_body, nbt=nbt, tc=tc),
        grid=(bsz // nbt, seq_len // tc),
        in_specs=[tok] * 6 + [st],
        out_specs=[tok, st],
        out_shape=[jax.ShapeDtypeStruct((bsz, seq_len, w), F32),
                   jax.ShapeDtypeStruct((bsz, A_PAIRS, A_HEAD, LANES), F32)],
        scratch_shapes=[pltpu.VMEM((nbt, A_PAIRS, A_HEAD, LANES), F32),
                        pltpu.VMEM((nbt * A_PAIRS * A_HEAD, min(SUBLANES, tc) * LANES), BF16)],
        compiler_params=_cparams(("parallel", "arbitrary")),
        name="rwkv_scan",
    )(r, d, k, v, kk, b, s0)


def _rwkv_post_body(o_ref, bonus_ref, g_ref, lnw_ref, lnb_ref, out_ref):
    o = o_ref[...]
    ones = _block_ones(F32)
    inv_n = 1.0 / A_HEAD
    mean = _seg_sum(o, ones) * inv_n
    cen = o - mean
    var = _seg_sum(cen * cen, ones) * inv_n
    y = cen * lax.rsqrt(var + A_GN_EPS) * lnw_ref[...] + lnb_ref[...]
    out_ref[...] = ((y + bonus_ref[...]) * g_ref[...]).astype(out_ref.dtype)


def rwkv_post(o, bonus, g, ln_w, ln_b, tr=256):
    t, w = o.shape
    tr = min(tr, t)
    tok = pl.BlockSpec((tr, w), lambda i: (i, 0))
    vec = pl.BlockSpec((1, w), lambda i: (0, 0))
    return pl.pallas_call(
        _rwkv_post_body,
        grid=(pl.cdiv(t, tr),),
        in_specs=[tok, tok, tok, vec, vec],
        out_specs=tok,
        out_shape=jax.ShapeDtypeStruct((t, w), BF16),
        compiler_params=_cparams(("parallel",)),
        name="rwkv_post",
    )(o, bonus, g, ln_w.reshape(1, w), ln_b.reshape(1, w))


def _rope_body(xa_ref, xb_ref, cos_ref, sa_ref, sb_ref, q_ref, k_ref, v_ref, k16_ref, v16_ref, *, shift, nblk):
    cos = cos_ref[...]
    sin_a = sa_ref[...]
    sin_b = sb_ref[...]
    half = B_ROT // 2
    nslab = B_WIDTH // LANES
    lane = lax.broadcasted_iota(jnp.int32, cos.shape, 1)

    def block(i):
        ref, i = (xa_ref, i) if i < nblk else (xb_ref, i - nblk)
        x = ref[0, :, i * LANES:(i + 1) * LANES]
        return pltpu.roll(x, LANES - shift, 1) if shift else x

    nxt = block(0)
    for s in range(3 * nslab):
        if shift:
            cur, nxt = nxt, block(s + 1)
            x = jnp.where(lane < LANES - shift, cur, nxt)
        else:
            x = block(s)
        if s < 2 * nslab:
            x = x * cos + pltpu.roll(x, half, 1) * sin_a + pltpu.roll(x, LANES - half, 1) * sin_b
        if s < nslab:
            q_ref[0, :, s * LANES:(s + 1) * LANES] = (x * (B_QK ** -0.5)).astype(q_ref.dtype)
        elif s < 2 * nslab:
            k_ref[0, :, (s - nslab) * LANES:(s - nslab + 1) * LANES] = x
            k16_ref[0, :, (s - nslab) * LANES:(s - nslab + 1) * LANES] = x.astype(BF16)
        else:
            v_ref[0, :, (s - 2 * nslab) * LANES:(s - 2 * nslab + 1) * LANES] = x
            v16_ref[0, :, (s - 2 * nslab) * LANES:(s - 2 * nslab + 1) * LANES] = x.astype(BF16)


def rope_tables(pos):
    half = B_ROT // 2
    inv = ROPE_THETA ** (-jnp.arange(half, dtype=F32) * 2.0 / B_ROT)
    ang = pos.astype(F32)[:, None] * inv[None, :]
    cos, sin = jnp.cos(ang), jnp.sin(ang)
    n = pos.shape[0]
    one = jnp.ones((n, B_QK - B_ROT), F32)
    zero = jnp.zeros((n, B_QK - B_ROT), F32)
    zh = jnp.zeros((n, half), F32)
    comp = lambda a, b_, rest: jnp.concatenate([a, b_, rest], axis=1)
    cos_p = comp(cos, cos, one)
    sin_a = comp(zh, sin, zero)
    sin_b = comp(-sin, zh, zero)
    dup = lambda x: jnp.concatenate([x, x], axis=1)
    return dup(cos_p), dup(sin_a), dup(sin_b)


def rope_qkv(u, col0, tables, tm=256):
    bsz, seq_len, width = u.shape
    tm = min(tm, seq_len)
    if col0 == 0:
        nblk, first, shift = 3 * B_WIDTH // LANES, 0, 0
    else:
        nblk, first, shift = col0 // LANES, 1, col0 % LANES
        assert 2 * nblk * LANES >= shift + 3 * B_WIDTH + LANES and 3 * nblk * LANES <= width
    win = nblk * LANES
    tab = pl.BlockSpec((tm, LANES), lambda b, j: (j, 0))
    out_spec = pl.BlockSpec((1, tm, B_WIDTH), lambda b, j: (b, j, 0))
    sds = lambda dt: jax.ShapeDtypeStruct((bsz, seq_len, B_WIDTH), dt)
    return pl.pallas_call(
        functools.partial(_rope_body, shift=shift, nblk=nblk),
        grid=(bsz, seq_len // tm),
        in_specs=[pl.BlockSpec((1, tm, win), lambda b, j: (b, j, first)),
                  pl.BlockSpec((1, tm, win), lambda b, j: (b, j, first + 1 if shift else first)),
                  tab, tab, tab],
        out_specs=[out_spec] * 5,
        out_shape=[sds(BF16), sds(F32), sds(F32), sds(BF16), sds(BF16)],
        compiler_params=_cparams(("parallel", "parallel")),
        name="rope_qkv",
    )(u, u, *tables)


def _lam(lqk_ref, lam_init):
    d1 = jnp.sum(lqk_ref[0:1, :] * lqk_ref[1:2, :], axis=-1, keepdims=True)
    d2 = jnp.sum(lqk_ref[2:3, :] * lqk_ref[3:4, :], axis=-1, keepdims=True)
    return jnp.exp(d1) - jnp.exp(d2) + lam_init


def _diff_finish(acc1, l1, acc2, l2, lam, subln, lam_init):
    a = acc1 / l1 - lam * (acc2 / l2)
    y = a * lax.rsqrt(jnp.mean(a * a, axis=-1, keepdims=True) + EPS) * subln
    return y * (1.0 - lam_init)


_NT = (((1,), (1,)), ((), ()))


def _attn_prompt_body(q_ref, k_ref, v_ref, lqk_ref, sub_ref, o_ref, qs_ref, m_ref, l_ref, acc_ref,
                      *, tq, tk, hp, lam_init):
    qi = pl.program_id(2)
    ki = pl.program_id(3)
    nchain = 2 * hp
    head = lambda h: slice(h * B_VDIM, (h + 1) * B_VDIM)

    @pl.when(ki == 0)
    def _():
        lane = lax.broadcasted_iota(jnp.int32, (tq, B_VDIM), 1)
        for h in range(hp):
            qh = q_ref[0, :, head(h)]
            qs_ref[2 * h] = jnp.where(lane < B_QK, qh, jnp.zeros_like(qh))
            qs_ref[2 * h + 1] = jnp.where(lane >= B_QK, qh, jnp.zeros_like(qh))
        m_ref[...] = jnp.full(m_ref.shape, NEG_INF, F32)
        l_ref[...] = jnp.zeros(l_ref.shape, F32)
        acc_ref[...] = jnp.zeros(acc_ref.shape, F32)

    def block(masked):
        ks = [k_ref[0, :, head(h)] for h in range(hp)]
        vs = [v_ref[0, :, head(h)] for h in range(hp)]
        ss = [lax.dot_general(qs_ref[c], ks[c // 2], _NT, preferred_element_type=F32) for c in range(nchain)]
        if masked:
            row = qi * tq + lax.broadcasted_iota(jnp.int32, (tq, tk), 0)
            col = ki * tk + lax.broadcasted_iota(jnp.int32, (tq, tk), 1)
            keep = col <= row
            ss = [jnp.where(keep, s, NEG_INF) for s in ss]
        alphas, ps = [], []
        for c, s in enumerate(ss):
            m_prev = m_ref[c]
            m_new = jnp.maximum(m_prev, jnp.max(s, axis=-1, keepdims=True))
            alpha = jnp.exp(m_prev - m_new)
            p = jnp.exp(s - m_new)
            l_ref[c] = alpha * l_ref[c] + jnp.sum(p, axis=-1, keepdims=True)
            m_ref[c] = m_new
            alphas.append(alpha)
            ps.append(p.astype(BF16))
        for c in range(nchain):
            acc_ref[c] = alphas[c] * acc_ref[c] + jnp.dot(ps[c], vs[c // 2], preferred_element_type=F32)

    visible = ki * tk < (qi + 1) * tq
    full = ki * tk + tk - 1 <= qi * tq

    @pl.when(full)
    def _():
        block(False)

    @pl.when(visible & jnp.logical_not(full))
    def _():
        block(True)

    @pl.when(ki == pl.num_programs(3) - 1)
    def _():
        lam = _lam(lqk_ref, lam_init)
        for h in range(hp):
            y = _diff_finish(acc_ref[2 * h], l_ref[2 * h], acc_ref[2 * h + 1], l_ref[2 * h + 1], lam,
                             sub_ref[...], lam_init)
            o_ref[0, :, head(h)] = y.astype(o_ref.dtype)


def attn_prompt(q, k, v, lqk, subln, lam_init, tq=256, tk=1024, hp=2):
    bsz, seq_len, _ = q.shape
    tq = min(tq, seq_len)
    tk = min(tk, seq_len)
    assert seq_len % tq == 0 and seq_len % tk == 0 and B_HEADS % hp == 0
    nq, nk = seq_len // tq, seq_len // tk
    ngrp = B_HEADS // hp
    kv_blk = lambda qi, ki: jnp.minimum(ki, ((qi + 1) * tq - 1) // tk)
    wide = hp * B_VDIM
    return pl.pallas_call(
        functools.partial(_attn_prompt_body, tq=tq, tk=tk, hp=hp, lam_init=lam_init),
        grid=(bsz, ngrp, nq, nk),
        in_specs=[
            pl.BlockSpec((1, tq, wide), lambda b, g, qi, ki: (b, qi, g)),
            pl.BlockSpec((1, tk, wide), lambda b, g, qi, ki: (b, kv_blk(qi, ki), g)),
            pl.BlockSpec((1, tk, wide), lambda b, g, qi, ki: (b, kv_blk(qi, ki), g)),
            pl.BlockSpec((4, B_QK), lambda b, g, qi, ki: (0, 0)),
            pl.BlockSpec((1, B_VDIM), lambda b, g, qi, ki: (0, 0)),
        ],
        out_specs=pl.BlockSpec((1, tq, wide), lambda b, g, qi, ki: (b, qi, g)),
        out_shape=jax.ShapeDtypeStruct((bsz, seq_len, B_WIDTH), BF16),
        scratch_shapes=[pltpu.VMEM((2 * hp, tq, B_VDIM), BF16), pltpu.VMEM((2 * hp, tq, 1), F32),
                        pltpu.VMEM((2 * hp, tq, 1), F32), pltpu.VMEM((2 * hp, tq, B_VDIM), F32)],
        compiler_params=_cparams(("parallel", "parallel", "parallel", "arbitrary")),
        name="attn_prompt",
    )(q, k, v, lqk, subln.reshape(1, B_VDIM))


def _attn_decode_body(pt_ref, q_ref, kn_ref, vn_ref, ck0_ref, ck1_ref, cv0_ref, cv1_ref, lqk_ref, sub_ref, o_ref,
                      qa_ref, m_ref, l_ref, acc_ref, bias_ref, *, nq, n_steps, lam_init):
    del pt_ref
    p = pl.program_id(1)
    rows = 2 * nq
    nrow = B_HEADS * rows
    bf_round = lambda x: x.astype(BF16).astype(F32)

    @pl.when(p == 0)
    def _():
        q = q_ref[0].astype(F32)
        lane = lax.broadcasted_iota(jnp.int32, (nq, B_VDIM), 1)
        for h in range(B_HEADS):
            qh = q[:, h * B_VDIM:(h + 1) * B_VDIM]
            qa_ref[h * rows:(h + 1) * rows, :] = jnp.concatenate(
                [jnp.where(lane < B_QK, qh, 0.0), jnp.where(lane >= B_QK, qh, 0.0)], axis=0)
        m_ref[...] = jnp.full(m_ref.shape, NEG_INF, F32)
        l_ref[...] = jnp.zeros(l_ref.shape, F32)
        acc_ref[...] = jnp.zeros(acc_ref.shape, F32)
        own = (lax.broadcasted_iota(jnp.int32, bias_ref.shape, 0) // rows
               == lax.broadcasted_iota(jnp.int32, bias_ref.shape, 1) % B_HEADS)
        bias_ref[...] = jnp.where(own, 0.0, NEG_INF)

    @pl.when(p < n_steps)
    def _():
        half = ck0_ref.shape[2] * B_HEADS
        flat = lambda ref: ref[0, 0].reshape(half, B_VDIM).astype(BF16)
        k = jnp.concatenate([flat(ck0_ref), flat(ck1_ref)], axis=0)
        v = jnp.concatenate([flat(cv0_ref), flat(cv1_ref)], axis=0)
        s = lax.dot_general(qa_ref[...].astype(BF16), k, _NT, preferred_element_type=F32) + bias_ref[...]
        m_prev = m_ref[...]
        m_new = jnp.maximum(m_prev, jnp.max(s, axis=-1, keepdims=True))
        alpha = jnp.exp(m_prev - m_new)
        pr = jnp.exp(s - m_new)
        l_ref[...] = alpha * l_ref[...] + jnp.sum(pr, axis=-1, keepdims=True)
        acc_ref[...] = alpha * acc_ref[...] + jnp.dot(pr.astype(BF16), v, preferred_element_type=F32)
        m_ref[...] = m_new

    @pl.when(p == n_steps)
    def _():
        lam = _lam(lqk_ref, lam_init)
        qrow = lax.broadcasted_iota(jnp.int32, (nrow, 1), 0) % nq
        qa = bf_round(qa_ref[...])
        kn = bf_round(kn_ref[0])
        vn = bf_round(vn_ref[0])
        per_head = lambda x, j: jnp.concatenate(
            [jnp.broadcast_to(x[j:j + 1, h * B_VDIM:(h + 1) * B_VDIM], (rows, B_VDIM)) for h in range(B_HEADS)], axis=0)
        ss = [jnp.where(qrow >= j, jnp.sum(qa * per_head(kn, j), axis=-1, keepdims=True), NEG_INF)
              for j in range(nq)]
        m_prev = m_ref[...]
        m_new = m_prev
        for sj in ss:
            m_new = jnp.maximum(m_new, sj)
        alpha = jnp.exp(m_prev - m_new)
        l = alpha * l_ref[...]
        acc = alpha * acc_ref[...]
        for j, sj in enumerate(ss):
            pj = jnp.exp(sj - m_new)
            l = l + pj
            acc = acc + bf_round(pj) * per_head(vn, j)
        n = acc / l
        a = n - lam * pltpu.roll(n, nrow - nq, 0)
        y = a * lax.rsqrt(jnp.mean(a * a, axis=-1, keepdims=True) + EPS) * sub_ref[...] * (1.0 - lam_init)
        for h in range(B_HEADS):
            o_ref[0, :, h * B_VDIM:(h + 1) * B_VDIM] = y[h * rows:h * rows + nq].astype(o_ref.dtype)


def attn_decode(q, k, v, cache_k, cache_v, page_table, layer, lqk, subln, lam_init):
    bsz, nq, _ = q.shape
    n_pages = page_table.shape[1]
    page = cache_k.shape[2]
    nrow = B_HEADS * 2 * nq
    assert n_pages % 2 == 0
    n_steps = n_pages // 2
    cache_spec = lambda odd: pl.BlockSpec(
        (1, 1, page, B_HEADS, B_VDIM),
        lambda b, p, pt: (layer, pt[b * n_pages + 2 * jnp.minimum(p, n_steps - 1) + odd], 0, 0, 0))
    grid_spec = pltpu.PrefetchScalarGridSpec(
        num_scalar_prefetch=1,
        grid=(bsz, n_steps + 1),
        in_specs=[
            pl.BlockSpec((1, nq, B_WIDTH), lambda b, p, pt: (b, 0, 0)),
            pl.BlockSpec((1, nq, B_WIDTH), lambda b, p, pt: (b, 0, 0)),
            pl.BlockSpec((1, nq, B_WIDTH), lambda b, p, pt: (b, 0, 0)),
            cache_spec(0), cache_spec(1), cache_spec(0), cache_spec(1),
            pl.BlockSpec((4, B_QK), lambda b, p, pt: (0, 0)),
            pl.BlockSpec((1, B_VDIM), lambda b, p, pt: (0, 0)),
        ],
        out_specs=pl.BlockSpec((1, nq, B_WIDTH), lambda b, p, pt: (b, 0, 0)),
        scratch_shapes=[
            pltpu.VMEM((nrow, B_VDIM), F32),
            pltpu.VMEM((nrow, 1), F32),
            pltpu.VMEM((nrow, 1), F32),
            pltpu.VMEM((nrow, B_VDIM), F32),
            pltpu.VMEM((nrow, 2 * page * B_HEADS), F32),
        ],
    )
    return pl.pallas_call(
        functools.partial(_attn_decode_body, nq=nq, n_steps=n_steps, lam_init=lam_init),
        grid_spec=grid_spec,
        out_shape=jax.ShapeDtypeStruct((bsz, nq, B_WIDTH), F32),
        compiler_params=_cparams(("parallel", "arbitrary")),
        name="attn_decode",
    )(page_table.reshape(-1), q, k, v, cache_k, cache_k, cache_v, cache_v, lqk, subln.reshape(1, B_VDIM)).astype(BF16)


_TN = (((0,), (0,)), ((), ()))


def _hgrn_body(qa_ref, qb_ref, fa_ref, fb_ref, ia_ref, ib_ref, ga_ref, gb_ref, lb_ref, cn_ref, s0_ref,
               o_ref, sfin_ref, s_ref, *, tc, n_valid, shift):
    c = pl.program_id(2)
    lane_in = lax.broadcasted_iota(jnp.int32, (tc, LANES), 1)

    def slab(a_ref, b_ref):
        if not shift:
            return a_ref[0]
        return jnp.where(lane_in < LANES - shift, pltpu.roll(a_ref[0], LANES - shift, 1),
                         pltpu.roll(b_ref[0], LANES - shift, 1))

    @pl.when(c == 0)
    def _():
        s_ref[...] = s0_ref[0, 0]

    nrow = -(-tc // C_CHUNK) * C_CHUNK
    pad = lambda x: x if nrow == tc else jnp.concatenate([x, jnp.zeros((nrow - tc, x.shape[1]), F32)], axis=0)
    q = _silu(pad(slab(qa_ref, qb_ref)))
    lb = lb_ref[...]
    fs = lb + (1.0 - lb) * _sigmoid(pad(slab(fa_ref, fb_ref)))
    kin = 1.0 - fs
    gl = jnp.log(fs)
    iv = pad(slab(ia_ref, ib_ref))
    row = lax.broadcasted_iota(jnp.int32, (nrow, C_DK), 0)
    if n_valid < nrow:
        kin = jnp.where(row < n_valid, kin, 0.0)
        gl = jnp.where(row < n_valid, gl, 0.0)
    b = gl
    sh = 1
    while sh < C_CHUNK:
        b = b + jnp.where(row % C_CHUNK >= sh, pltpu.roll(b, sh, 0), 0.0)
        sh *= 2
    trow = lax.broadcasted_iota(jnp.int32, (C_CHUNK, C_DK), 0)
    eye = (lax.broadcasted_iota(jnp.int32, (C_DK, C_DK), 0)
           == lax.broadcasted_iota(jnp.int32, (C_DK, C_DK), 1)).astype(F32)
    outs = []
    for ch in range(nrow // C_CHUNK):
        rs = slice(ch * C_CHUNK, (ch + 1) * C_CHUNK)
        bc, qc, kc, ic = b[rs], q[rs], kin[rs], iv[rs]
        s = s_ref[...]
        o = jnp.dot((qc * jnp.exp(bc)).astype(BF16), s.astype(BF16), preferred_element_type=F32)
        for sidx in range(C_CHUNK):
            dec = jnp.exp(jnp.where(trow >= sidx, bc - bc[sidx:sidx + 1, :], NEG_INF))
            att = jnp.sum(qc * kc[sidx:sidx + 1, :] * dec, axis=-1, keepdims=True)
            o = o + att * ic[sidx:sidx + 1, :]
        bl = bc[C_CHUNK - 1:C_CHUNK, :]
        kd = kc * jnp.exp(bl - bc)
        inc = lax.dot_general(kd.astype(BF16), ic.astype(BF16), _TN, preferred_element_type=F32)
        ecol = jnp.sum(eye * jnp.exp(bl), axis=1, keepdims=True)
        s_ref[...] = s * ecol + inc
        outs.append(o)
    o = jnp.concatenate(outs, axis=0)[0:tc]
    y = o * lax.rsqrt(jnp.mean(o * o, axis=-1, keepdims=True) + EPS) * cn_ref[...]
    o_ref[0] = (y * _silu(slab(ga_ref, gb_ref))).astype(o_ref.dtype)

    @pl.when(c == pl.num_programs(2) - 1)
    def _():
        sfin_ref[0, 0] = s_ref[...]


def hgrn(u, col0, s0, lb, c_norm, tc=256):
    bsz, seq_len, _ = u.shape
    tc = min(tc, seq_len)
    assert seq_len % tc == 0
    blk0, shift = col0 // LANES, col0 % LANES
    tok = lambda off: pl.BlockSpec((1, tc, LANES), lambda b, h, c: (b, c, blk0 + off + h))
    pair = lambda grp: [tok(grp * C_HEADS), tok(grp * C_HEADS + (1 if shift else 0))]
    st = pl.BlockSpec((1, 1, C_DK, C_DV), lambda b, h, c: (b, h, 0, 0))
    return pl.pallas_call(
        functools.partial(_hgrn_body, tc=tc, n_valid=tc, shift=shift),
        grid=(bsz, C_HEADS, seq_len // tc),
        in_specs=pair(0) + pair(1) + pair(2) + pair(3) + [
            pl.BlockSpec((1, LANES), lambda b, h, c: (0, h)),
            pl.BlockSpec((1, C_DV), lambda b, h, c: (0, 0)),
            st],
        out_specs=[pl.BlockSpec((1, tc, LANES), lambda b, h, c: (b, c, h)), st],
        out_shape=[jax.ShapeDtypeStruct((bsz, seq_len, C_WIDTH), BF16),
                   jax.ShapeDtypeStruct((bsz, C_HEADS, C_DK, C_DV), F32)],
        scratch_shapes=[pltpu.VMEM((C_DK, C_DV), F32)],
        compiler_params=_cparams(("parallel", "parallel", "arbitrary")),
        name="hgrn",
    )(*([u] * 8), lb.reshape(1, C_FK), c_norm.reshape(1, C_DV), s0)


def _pair_state(s):
    bsz = s.shape[0]
    return s.reshape(bsz, A_PAIRS, 2, A_HEAD, A_HEAD).transpose(0, 1, 3, 2, 4).reshape(bsz, A_PAIRS, A_HEAD, LANES)


def _unpair_state(s):
    bsz = s.shape[0]
    return s.reshape(bsz, A_PAIRS, A_HEAD, 2, A_HEAD).transpose(0, 1, 3, 2, 4).reshape(bsz, A_HEADS, A_HEAD, A_HEAD)


def _layer_params(l, P):
    padc = A_COLS_PAD - A_COLS
    w2 = jnp.zeros((LANES, 2 * A_WIDTH), F32)
    w2 = w2.at[0:A_DECAY_R, 0:A_WIDTH].set(P["a_w_up"][l])
    w2 = w2.at[A_DECAY_R:A_DECAY_R + A_ICLR_R, A_WIDTH:].set(P["a_a_up"][l])
    return dict(
        mu=jnp.pad(P["a_mu"][l], (0, padc)),
        w2=w2,
        gup=jnp.pad(P["a_g_up"][l], ((0, 2 * LANES - A_GATE_R), (0, 0))),
        lqk=jnp.stack([P["b_lq1"][l], P["b_lk1"][l], P["b_lq2"][l], P["b_lk2"][l]]),
        r_k=P["a_r_k"][l].reshape(A_WIDTH),
    )


def _run_trunk(x, pos, caches, wkv0, shift0, hgrn0, conv0, lbs, P, LP):
    bsz, seq_len, d = x.shape
    t = bsz * seq_len
    h = x.reshape(t, d)
    tables = rope_tables(pos)
    padc = A_COLS_PAD - A_COLS
    w_in_t = jnp.swapaxes(P["w_in"], 1, 2)
    ks, vs, wkvs, shifts, hgrns, convs = [], [], [], [], [], []
    for l in range(DEPTH):
        lp = LP[l]
        hn = rmsnorm_rows(h, P["norm_mix"][l], BF16)
        u = matmul(hn, w_in_t, layer=l, pad_n=True, tn=512, w_rows_are_n=True)
        u = u.reshape(bsz, seq_len, u.shape[-1])
        r, dcy, kp, v, kk, bb, g, bonus = rwkv_pre(
            u, jnp.pad(shift0[l], ((0, 0), (0, padc))), lp["mu"], lp["w2"], lp["gup"],
            P["a_w0"][l], P["a_a0"][l], P["a_k_k"][l], P["a_k_a"][l], lp["r_k"])
        o_a, s_a = rwkv_scan(r, dcy, kp, v, kk, bb, _pair_state(wkv0[l]))
        flat = lambda z: z.reshape(t, A_WIDTH)
        oa = rwkv_post(flat(o_a), flat(bonus), flat(g), P["a_ln_w"][l], P["a_ln_b"][l])
        lam_init = 0.8 - 0.6 * math.exp(-0.3 * l)
        qb, kb, vb, kb16, vb16 = rope_qkv(u, A_COLS, tables)
        if caches is None:
            ob = attn_prompt(qb, kb16, vb16, lp["lqk"], P["b_subln"][l], lam_init)
        else:
            ob = attn_decode(qb, kb, vb, caches[0], caches[1], caches[2], l, lp["lqk"], P["b_subln"][l], lam_init)
        oc, s_c = hgrn(u, A_COLS + 3 * B_WIDTH, hgrn0[l], lbs[l], P["c_norm"][l])
        mix = jnp.concatenate([oa, ob.reshape(t, B_WIDTH), oc.reshape(t, C_WIDTH)], axis=-1)
        h = matmul(mix, P["w_out"], layer=l, res=h)
        hf = rmsnorm_rows(h, P["norm_ffn"][l], BF16)
        mid, buf = ffn_up(hf, P["f_gate"], P["f_up"], P["f_conv"], P["f_conv_b"], conv0[l], seq_len, layer=l)
        h = matmul_kloop(mid, P["f_down"], l, h)
        ks.append(kb.reshape(bsz, seq_len, B_HEADS, B_VDIM))
        vs.append(vb.reshape(bsz, seq_len, B_HEADS, B_VDIM))
        wkvs.append(_unpair_state(s_a))
        shifts.append(u[:, -1, :A_COLS])
        hgrns.append(s_c)
        convs.append(buf)
    y = rmsnorm_rows(h, P["norm_final"], F32).reshape(bsz, seq_len, d)
    st = jnp.stack
    return y, st(ks), st(vs), st(wkvs), st(shifts), st(hgrns), st(convs)


def kernel(x_prompt, x_sample, cache_k, cache_v, page_table, state_wkv, state_shift, state_hgrn, state_conv,
           norm_mix, w_in, a_mu, a_w0, a_w_up, a_a0, a_a_up, a_g_up, a_k_k, a_k_a, a_r_k, a_ln_w, a_ln_b,
           b_lq1, b_lk1, b_lq2, b_lk2, b_subln, c_lb_logits, c_norm, w_out, norm_ffn, f_gate, f_up,
           f_conv, f_conv_b, f_down, norm_final):
    P = dict(norm_mix=norm_mix, w_in=w_in, a_mu=a_mu, a_w0=a_w0, a_w_up=a_w_up, a_a0=a_a0, a_a_up=a_a_up,
             a_g_up=a_g_up, a_k_k=a_k_k, a_k_a=a_k_a, a_r_k=a_r_k, a_ln_w=a_ln_w, a_ln_b=a_ln_b,
             b_lq1=b_lq1, b_lk1=b_lk1, b_lq2=b_lq2, b_lk2=b_lk2, b_subln=b_subln, c_norm=c_norm,
             w_out=w_out, norm_ffn=norm_ffn, f_gate=f_gate, f_up=f_up, f_conv=f_conv, f_conv_b=f_conv_b,
             f_down=f_down, norm_final=norm_final)
    LP = [_layer_params(l, P) for l in range(DEPTH)]
    lbs = lower_bounds(c_lb_logits)
    bp, lp_ = x_prompt.shape[:2]
    past_len = page_table.shape[1] * cache_k.shape[2]
    pos_p = jnp.arange(lp_)
    pos_s = past_len + jnp.arange(x_sample.shape[1])
    zeros = lambda *s: jnp.zeros(s, F32)
    out_p = _run_trunk(
        x_prompt, pos_p, None,
        zeros(DEPTH, bp, A_HEADS, A_HEAD, A_HEAD), zeros(DEPTH, bp, A_COLS),
        zeros(DEPTH, bp, C_HEADS, C_DK, C_DV), zeros(DEPTH, bp, CONV_W - 1, D_FF), lbs, P, LP)
    caches = (cache_k, cache_v, page_table)
    out_s = _run_trunk(x_sample, pos_s, caches, state_wkv, state_shift, state_hgrn, state_conv, lbs, P, LP)
    y_p, k_p, v_p, wkv_p, sh_p, hg_p, cv_p = out_p
    y_s, k_s, v_s, wkv_s, sh_s, hg_s, cv_s = out_s
    return (y_p, y_s, k_p, v_p, k_s, v_s, wkv_p, wkv_s, sh_p, sh_s, hg_p, hg_s, cv_p, cv_s)
```
